```python
import math
import jax
import jax.numpy as jnp
from jax import lax
import numpy as np

D_MODEL = 1024
BATCH = 1
SEQ = 16384
DEPTH = 2
DEC_BATCH = 16
DEC_SEQ = 64
PAST_LEN = 4096

CHUNK = 64
N_AB = (DEPTH + 1) // 2
N_CD = DEPTH // 2
RMS_EPS = 1e-6
LN_EPS = 1e-5

RET_HEADS = 4
RET_DK = 128
RET_DV = 128
RET_WIDTH = RET_HEADS * RET_DV
RET_THETA = 10000.0

RWKV_HEADS = 8
RWKV_HD = 64
RWKV_WIDTH = RWKV_HEADS * RWKV_HD
RWKV_W_LORA = 64
RWKV_A_LORA = 64
RWKV_G_LORA = 128
RWKV_PROJ = 3 * RWKV_WIDTH + RWKV_W_LORA + RWKV_A_LORA + RWKV_G_LORA
RWKV_LN_EPS = 64e-5

CONV_CH = 512
CONV_WIDTH = 31

ATT_HEADS = 8
ATT_KV_HEADS = 2
ATT_HD = 64
ATT_GROUP = ATT_HEADS // ATT_KV_HEADS
WINDOW = 128
ROT_DIM = ATT_HD // 4
ROPE_THETA = 500000.0
NEG_INF = -1e30

D_FF = -(-8 * D_MODEL // (3 * 256)) * 256

IN_AB = 2 * RET_HEADS * RET_DK + 2 * RET_WIDTH + RWKV_PROJ
MIX_AB = RET_WIDTH + RWKV_WIDTH
IN_CD = 2 * CONV_CH + (ATT_HEADS + 2 * ATT_KV_HEADS) * ATT_HD
MIX_CD = CONV_CH + ATT_HEADS * ATT_HD

kernel_name = 'hybrid_stream_ret_rwkv7_conformer_swa'


def rms_norm(x, g):
    xf = x.astype(jnp.float32)
    y = xf * lax.rsqrt(jnp.mean(xf * xf, axis=-1, keepdims=True) + RMS_EPS)
    return (y * g.astype(jnp.float32)).astype(x.dtype)


def layer_norm(x, g, b, eps):
    xf = x.astype(jnp.float32)
    xc = xf - jnp.mean(xf, axis=-1, keepdims=True)
    y = xc * lax.rsqrt(jnp.mean(xc * xc, axis=-1, keepdims=True) + eps)
    return (y * g.astype(jnp.float32) + b.astype(jnp.float32)).astype(x.dtype)


def head_norm(x, g, eps):
    xf = x.astype(jnp.float32)
    xc = xf - jnp.mean(xf, axis=-1, keepdims=True)
    y = xc * lax.rsqrt(jnp.mean(xc * xc, axis=-1, keepdims=True) + eps)
    y = y.reshape(x.shape[0], x.shape[1], -1) * g.astype(jnp.float32)
    return y.astype(x.dtype)


def rope(x, pos, rot_dim, theta):
    half = rot_dim // 2
    inv = jnp.exp(jnp.arange(half, dtype=jnp.float32) * (-2.0 * math.log(theta) / rot_dim))
    ang = pos.astype(jnp.float32)[:, None] * inv[None, :]
    cos = jnp.cos(ang)[:, None, :].astype(x.dtype)
    sin = jnp.sin(ang)[:, None, :].astype(x.dtype)
    x1 = x[..., :half]
    x2 = x[..., half:rot_dim]
    return jnp.concatenate([x1 * cos - x2 * sin, x1 * sin + x2 * cos, x[..., rot_dim:]], axis=-1)


def retention(q, k, v, r0):
    B, L, H, _ = q.shape
    C = min(CHUNK, L)
    n = L // C
    lg = jnp.log1p(-jnp.exp2(-5.0 - jnp.arange(H, dtype=jnp.float32)))
    idx = jnp.arange(C, dtype=jnp.float32)
    diff = idx[:, None] - idx[None, :]
    dmat = jnp.where(diff >= 0, jnp.exp(lg[:, None, None] * jnp.maximum(diff, 0.0)), 0.0).astype(v.dtype)
    q_dec = jnp.exp(lg[:, None] * (idx + 1.0)).astype(v.dtype)
    k_dec = jnp.exp(lg[:, None] * (C - 1.0 - idx)).astype(v.dtype)
    c_dec = jnp.exp(lg * C).astype(v.dtype)

    def blocks(t):
        return t.reshape(B, n, C, H, t.shape[-1]).transpose(1, 0, 3, 2, 4)

    def step(r, blk):
        qc, kc, vc = blk
        inner = jnp.einsum('bhqd,bhkd->bhqk', qc, kc) * dmat
        o = jnp.einsum('bhqk,bhke->bhqe', inner, vc) + jnp.einsum('bhqd,bhde->bhqe', qc, r) * q_dec[..., None]
        r = r * c_dec[:, None, None] + jnp.einsum('bhkd,bhke->bhde', kc * k_dec[..., None], vc)
        return r, o

    r_fin, o = lax.scan(step, r0, (blocks(q), blocks(k), blocks(v)))
    return o.transpose(1, 0, 3, 2, 4).reshape(B, L, H, -1), r_fin


def rwkv7_mix(p, prev, s0, mu, w0, w2, a0, a2, g2, k_k, k_a, r_k, ln_g, ln_b):
    B, L, _ = p.shape
    H, N, W = RWKV_HEADS, RWKV_HD, RWKV_WIDTH
    p_prev = jnp.concatenate([prev[:, None, :], p[:, :-1]], axis=1)
    xs = p + (p_prev - p) * mu
    o_w = 3 * W
    o_a = o_w + RWKV_W_LORA
    o_g = o_a + RWKV_A_LORA
    r, k, v = xs[..., :W], xs[..., W:2 * W], xs[..., 2 * W:o_w]
    w_log = -jax.nn.softplus(-(w0 + jnp.tanh(xs[..., o_w:o_a]) @ w2)) - 0.5
    decay = jnp.exp(-jnp.exp(w_log))
    a = jax.nn.sigmoid(a0 + xs[..., o_a:o_g] @ a2)
    g = jax.nn.sigmoid(xs[..., o_g:]) @ g2
    kkf = (k * k_k).reshape(B, L, H, N).astype(jnp.float32)
    kk = (kkf / jnp.maximum(jnp.sqrt(jnp.sum(kkf * kkf, axis=-1, keepdims=True)), 1e-12)).astype(p.dtype)
    k = k * (1.0 + (a - 1.0) * k_a)

    def heads(t):
        return t.reshape(B, L, H, N)

    def tmajor(t):
        return heads(t).transpose(1, 0, 2, 3)

    def step(S, inp):
        r_t, w_t, k_t, v_t, kk_t, a_t = inp
        sa = jnp.einsum('bhij,bhj->bhi', S, -kk_t)
        S = S * w_t[:, :, None, :] + sa[..., None] * (kk_t * a_t)[:, :, None, :] + v_t[..., None] * k_t[:, :, None, :]
        return S, jnp.einsum('bhij,bhj->bhi', S, r_t)

    s_fin, y = lax.scan(step, s0, (tmajor(r), tmajor(decay), tmajor(k), tmajor(v), kk.transpose(1, 0, 2, 3), tmajor(a)))
    y = head_norm(y.transpose(1, 0, 2, 3), ln_g, RWKV_LN_EPS) + ln_b
    bonus = jnp.sum(heads(r) * heads(k) * r_k, axis=-1, keepdims=True) * heads(v)
    y = (y + bonus.reshape(B, L, W)) * g
    return y, s_fin, p[:, -1]


def conv_mix(cu, buf, conv_w, conv_b, ln_g, ln_b):
    u = cu[..., :CONV_CH] * jax.nn.sigmoid(cu[..., CONV_CH:])
    up = jnp.concatenate([buf, u], axis=1)
    y = lax.conv_general_dilated(up, conv_w[:, None, :], window_strides=(1,), padding='VALID',
                                 dimension_numbers=('NWC', 'WIO', 'NWC'), feature_group_count=CONV_CH) + conv_b
    y = jax.nn.silu(layer_norm(y, ln_g, ln_b, LN_EPS))
    return y, up[:, up.shape[1] - (CONV_WIDTH - 1):]


def swa_attention(q, k, v, past_k, past_v, sink, has_past):
    B, L = q.shape[0], q.shape[1]
    P = past_k.shape[1]
    C = min(CHUNK, L)
    n = L // C
    kp = jnp.concatenate([past_k, k], axis=1)
    vp = jnp.concatenate([past_v, v], axis=1)
    idx = (jnp.arange(n) * C)[:, None] + jnp.arange(P + C)[None, :]
    kb = kp[:, idx]
    vb = vp[:, idx]
    qb = q.reshape(B, n, C, ATT_KV_HEADS, ATT_GROUP, ATT_HD)
    s = jnp.einsum('bnqkgd,bnskd->bnkgqs', qb, kb).astype(jnp.float32) * (ATT_HD ** -0.5)
    if not has_past:
        s = jnp.where((idx >= P)[None, :, None, None, None, :], s, NEG_INF)
    sk = sink.astype(jnp.float32).reshape(1, 1, ATT_KV_HEADS, ATT_GROUP, 1, 1)
    m = jnp.maximum(jnp.max(s, axis=-1, keepdims=True), sk)
    e = jnp.exp(s - m)
    pr = e / (jnp.sum(e, axis=-1, keepdims=True) + jnp.exp(sk - m))
    o = jnp.einsum('bnkgqs,bnskd->bnqkgd', pr.astype(v.dtype), vb)
    return o.reshape(B, L, ATT_HEADS * ATT_HD), kp[:, L:], vp[:, L:]


def _trunk(x, pos0, has_past, st_ret, st_wkv, st_shift, st_conv, st_k, st_v, prm):
    B, L, _ = x.shape
    pos = pos0 + jnp.arange(L, dtype=jnp.int32)
    new_ret, new_wkv, new_shift, new_conv, new_k, new_v = [], [], [], [], [], []
    for layer in range(DEPTH):
        i = layer // 2
        h = rms_norm(x, prm['norm_mix'][layer])
        if layer % 2 == 0:
            proj = h @ prm['w_in_ab'][i]
            qk = RET_HEADS * RET_DK
            o_v = 2 * qk
            o_gt = o_v + RET_WIDTH
            o_rw = o_gt + RET_WIDTH
            q = rope(proj[..., :qk].reshape(B, L, RET_HEADS, RET_DK), pos, RET_DK, RET_THETA)
            k = rope(proj[..., qk:o_v].reshape(B, L, RET_HEADS, RET_DK), pos, RET_DK, RET_THETA) * (RET_DK ** -0.5)
            v = proj[..., o_v:o_gt].reshape(B, L, RET_HEADS, RET_DV)
            o_ret, r_new = retention(q, k, v, st_ret[i])
            o_ret = jax.nn.silu(proj[..., o_gt:o_rw]) * head_norm(o_ret, prm['ret_gn_g'][i], LN_EPS)
            o_rwkv, s_new, sh_new = rwkv7_mix(proj[..., o_rw:], st_shift[i], st_wkv[i], prm['rwkv_mu'][i],
                                              prm['rwkv_w0'][i], prm['rwkv_w2'][i], prm['rwkv_a0'][i],
                                              prm['rwkv_a2'][i], prm['rwkv_g2'][i], prm['rwkv_kk'][i],
                                              prm['rwkv_ka'][i], prm['rwkv_rk'][i], prm['rwkv_ln_g'][i],
                                              prm['rwkv_ln_b'][i])
            x = x + jnp.concatenate([o_ret, o_rwkv], axis=-1) @ prm['w_out_ab'][i]
            new_ret.append(r_new)
            new_wkv.append(s_new)
            new_shift.append(sh_new)
        else:
            proj = h @ prm['w_in_cd'][i]
            o1 = 2 * CONV_CH
            o2 = o1 + ATT_HEADS * ATT_HD
            o3 = o2 + ATT_KV_HEADS * ATT_HD
            o_conv, c_new = conv_mix(proj[..., :o1], st_conv[i], prm['conv_w'][i], prm['conv_b'][i],
                                     prm['conv_ln_g'][i], prm['conv_ln_b'][i])
            q = rope(proj[..., o1:o2].reshape(B, L, ATT_HEADS, ATT_HD), pos, ROT_DIM, ROPE_THETA)
            k = rope(proj[..., o2:o3].reshape(B, L, ATT_KV_HEADS, ATT_HD), pos, ROT_DIM, ROPE_THETA)
            v = proj[..., o3:].reshape(B, L, ATT_KV_HEADS, ATT_HD)
            o_att, k_new, v_new = swa_attention(q, k, v, st_k[i], st_v[i], prm['attn_sink'][i], has_past)
            x = x + jnp.concatenate([o_conv, o_att], axis=-1) @ prm['w_out_cd'][i]
            new_conv.append(c_new)
            new_k.append(k_new)
            new_v.append(v_new)
        h = rms_norm(x, prm['norm_ffn'][layer])
        x = x + (jax.nn.silu(h @ prm['ffn_gate'][layer]) * (h @ prm['ffn_up'][layer])) @ prm['ffn_down'][layer]
    return (rms_norm(x, prm['norm_final']), jnp.stack(new_ret), jnp.stack(new_wkv), jnp.stack(new_shift),
            jnp.stack(new_conv), jnp.stack(new_k), jnp.stack(new_v))


def setup_inputs(seed: int = 0) -> dict:
    key = jax.random.key(seed)
    keys = jax.random.split(key, 40)
    counter = [0]

    def nk():
        counter[0] += 1
        return keys[counter[0] - 1]

    def nrm(shape, scale):
        return jax.random.normal(nk(), shape, jnp.float32) * scale

    def gain(shape):
        return 1.0 + nrm(shape, 0.01)

    p_rows = min(WINDOW, PAST_LEN)
    return {
        'x_prompt': nrm((BATCH, SEQ, D_MODEL), 1.0),
        'x_sample': nrm((DEC_BATCH, DEC_SEQ, D_MODEL), 1.0),
        'state_ret': nrm((N_AB, DEC_BATCH, RET_HEADS, RET_DK, RET_DV), 1.0),
        'state_wkv': nrm((N_AB, DEC_BATCH, RWKV_HEADS, RWKV_HD, RWKV_HD), 0.5),
        'state_shift': nrm((N_AB, DEC_BATCH, RWKV_PROJ), 1.0),
        'state_conv': nrm((N_CD, DEC_BATCH, CONV_WIDTH - 1, CONV_CH), 0.5),
        'cache_k': nrm((N_CD, DEC_BATCH, p_rows, ATT_KV_HEADS, ATT_HD), 1.0),
        'cache_v': nrm((N_CD, DEC_BATCH, p_rows, ATT_KV_HEADS, ATT_HD), 1.0),
        'norm_mix': gain((DEPTH, D_MODEL)),
        'norm_ffn': gain((DEPTH, D_MODEL)),
        'norm_final': gain((D_MODEL,)),
        'w_in_ab': nrm((N_AB, D_MODEL, IN_AB), D_MODEL ** -0.5),
        'w_out_ab': nrm((N_AB, MIX_AB, D_MODEL), MIX_AB ** -0.5),
        'ret_gn_g': gain((N_AB, RET_WIDTH)),
        'rwkv_mu': jax.random.uniform(nk(), (N_AB, RWKV_PROJ), jnp.float32, 0.0, 1.0),
        'rwkv_w0': jax.random.uniform(nk(), (N_AB, RWKV_WIDTH), jnp.float32, -6.0, -1.0),
        'rwkv_w2': nrm((N_AB, RWKV_W_LORA, RWKV_WIDTH), 0.1 * RWKV_W_LORA ** -0.5),
        'rwkv_a0': nrm((N_AB, RWKV_WIDTH), 0.1),
        'rwkv_a2': nrm((N_AB, RWKV_A_LORA, RWKV_WIDTH), 0.1 * RWKV_A_LORA ** -0.5),
        'rwkv_g2': nrm((N_AB, RWKV_G_LORA, RWKV_WIDTH), RWKV_G_LORA ** -0.5),
        'rwkv_kk': 0.85 + nrm((N_AB, RWKV_WIDTH), 0.01),
        'rwkv_ka': gain((N_AB, RWKV_WIDTH)),
        'rwkv_rk': nrm((N_AB, RWKV_HEADS, RWKV_HD), 0.1),
        'rwkv_ln_g': gain((N_AB, RWKV_WIDTH)),
        'rwkv_ln_b': nrm((N_AB, RWKV_WIDTH), 0.01),
        'w_in_cd': nrm((N_CD, D_MODEL, IN_CD), D_MODEL ** -0.5),
        'w_out_cd': nrm((N_CD, MIX_CD, D_MODEL), MIX_CD ** -0.5),
        'conv_w': nrm((N_CD, CONV_WIDTH, CONV_CH), CONV_WIDTH ** -0.5),
        'conv_b': nrm((N_CD, CONV_CH), 0.01),
        'conv_ln_g': gain((N_CD, CONV_CH)),
        'conv_ln_b': nrm((N_CD, CONV_CH), 0.01),
        'attn_sink': nrm((N_CD, ATT_HEADS), 0.5),
        'ffn_gate': nrm((DEPTH, D_MODEL, D_FF), D_MODEL ** -0.5),
        'ffn_up': nrm((DEPTH, D_MODEL, D_FF), D_MODEL ** -0.5),
        'ffn_down': nrm((DEPTH, D_FF, D_MODEL), D_FF ** -0.5),
    }


def reference(x_prompt, x_sample, state_ret, state_wkv, state_shift, state_conv, cache_k, cache_v,
              norm_mix, norm_ffn, norm_final, w_in_ab, w_out_ab, ret_gn_g, rwkv_mu, rwkv_w0, rwkv_w2,
              rwkv_a0, rwkv_a2, rwkv_g2, rwkv_kk, rwkv_ka, rwkv_rk, rwkv_ln_g, rwkv_ln_b, w_in_cd, w_out_cd,
              conv_w, conv_b, conv_ln_g, conv_ln_b, attn_sink, ffn_gate, ffn_up, ffn_down):
    prm = {
        'norm_mix': norm_mix, 'norm_ffn': norm_ffn, 'norm_final': norm_final,
        'w_in_ab': w_in_ab, 'w_out_ab': w_out_ab, 'ret_gn_g': ret_gn_g,
        'rwkv_mu': rwkv_mu, 'rwkv_w0': rwkv_w0, 'rwkv_w2': rwkv_w2, 'rwkv_a0': rwkv_a0, 'rwkv_a2': rwkv_a2,
        'rwkv_g2': rwkv_g2, 'rwkv_kk': rwkv_kk, 'rwkv_ka': rwkv_ka, 'rwkv_rk': rwkv_rk,
        'rwkv_ln_g': rwkv_ln_g, 'rwkv_ln_b': rwkv_ln_b,
        'w_in_cd': w_in_cd, 'w_out_cd': w_out_cd, 'conv_w': conv_w, 'conv_b': conv_b,
        'conv_ln_g': conv_ln_g, 'conv_ln_b': conv_ln_b, 'attn_sink': attn_sink,
        'ffn_gate': ffn_gate, 'ffn_up': ffn_up, 'ffn_down': ffn_down,
    }
    B = x_prompt.shape[0]
    dt = x_prompt.dtype
    kv_rows = cache_k.shape[2]
    z_ret = jnp.zeros((N_AB, B, RET_HEADS, RET_DK, RET_DV), dt)
    z_wkv = jnp.zeros((N_AB, B, RWKV_HEADS, RWKV_HD, RWKV_HD), dt)
    z_shift = jnp.zeros((N_AB, B, RWKV_PROJ), dt)
    z_conv = jnp.zeros((N_CD, B, CONV_WIDTH - 1, CONV_CH), dt)
    z_kv = jnp.zeros((N_CD, B, kv_rows, ATT_KV_HEADS, ATT_HD), dt)
    y_prompt, ret_p, wkv_p, shift_p, conv_p, k_p, v_p = _trunk(
        x_prompt, 0, False, z_ret, z_wkv, z_shift, z_conv, z_kv, z_kv, prm)
    y_sample, ret_s, wkv_s, shift_s, conv_s, k_s, v_s = _trunk(
        x_sample, PAST_LEN, True, state_ret, state_wkv, state_shift, state_conv, cache_k, cache_v, prm)
    return (y_prompt, y_sample, ret_p, wkv_p, shift_p, conv_p, k_p, v_p, ret_s, wkv_s, shift_s, conv_s, k_s, v_s)
```

```python
import functools
import math

import jax
import jax.numpy as jnp
import numpy as np
from jax import lax
from jax.experimental import pallas as pl
from jax.experimental.pallas import tpu as pltpu

F32 = jnp.float32
BF16 = jnp.bfloat16

D_MODEL = 1024
CHUNK = 64
RMS_EPS = 1e-6
LN_EPS = 1e-5

RET_HEADS = 4
RET_DK = 128
RET_DV = 128
RET_WIDTH = RET_HEADS * RET_DV
RET_THETA = 10000.0

RWKV_HEADS = 8
RWKV_HD = 64
RWKV_WIDTH = RWKV_HEADS * RWKV_HD
RWKV_W_LORA = 64
RWKV_A_LORA = 64
RWKV_G_LORA = 128
RWKV_PROJ = 3 * RWKV_WIDTH + RWKV_W_LORA + RWKV_A_LORA + RWKV_G_LORA
RWKV_LN_EPS = 64e-5

CONV_CH = 512
CONV_WIDTH = 31
CONV_PAD = 32

ATT_HEADS = 8
ATT_KV_HEADS = 2
ATT_HD = 64
ATT_GROUP = ATT_HEADS // ATT_KV_HEADS
WINDOW = 128
ROT_DIM = ATT_HD // 4
ROPE_THETA = 500000.0
NEG_INF = -1e30

V7X_VMEM_LIMIT_BYTES = 56 * 1024 * 1024

NN = (((1,), (0,)), ((), ()))
NT = (((1,), (1,)), ((), ()))
TN = (((0,), (0,)), ((), ()))


def _split_bf16(x, n):
    pieces = []
    rem = x
    for i in range(n):
        piece = rem.astype(BF16)
        pieces.append(piece)
        if i + 1 < n:
            rem = rem - piece.astype(F32)
    return pieces


def _mm(a, b, dn=NN, na=1, nb=1):
    pa = _split_bf16(a, na)
    pb = _split_bf16(b, nb)
    acc = None
    for i, x in enumerate(pa):
        for j, y in enumerate(pb):
            if i + j < max(na, nb):
                t = lax.dot_general(x, y, dn, preferred_element_type=F32)
                acc = t if acc is None else acc + t
    return acc


def _rms(x, g):
    ms = jnp.mean(x * x, axis=-1, keepdims=True)
    return (x * lax.rsqrt(ms + RMS_EPS)) * g


def _sigmoid(x):
    return 1.0 / (1.0 + jnp.exp(-x))


def _params(sem):
    return pltpu.CompilerParams(dimension_semantics=sem, vmem_limit_bytes=V7X_VMEM_LIMIT_BYTES)


def _const_spec(shape):
    nd = len(shape)
    return pl.BlockSpec(shape, lambda *_: (0,) * nd, pipeline_mode=pl.Buffered(1))


def _in_proj_kernel(x_ref, g_ref, wa_ref, wb_ref, oa_ref, ob_ref):
    h = _rms(x_ref[...], g_ref[...]).astype(BF16)
    oa_ref[...] = jnp.dot(h, wa_ref[...], preferred_element_type=F32)
    ob_ref[...] = jnp.dot(h, wb_ref[...], preferred_element_type=F32)


def _in_proj(x2, g, wa, wb, tm):
    t = x2.shape[0]
    na, nb = wa.shape[1], wb.shape[1]
    return pl.pallas_call(
        _in_proj_kernel,
        grid=(t // tm,),
        in_specs=[
            pl.BlockSpec((tm, D_MODEL), lambda i: (i, 0)),
            _const_spec((1, D_MODEL)),
            _const_spec((D_MODEL, na)),
            _const_spec((D_MODEL, nb)),
        ],
        out_specs=[
            pl.BlockSpec((tm, na), lambda i: (i, 0)),
            pl.BlockSpec((tm, nb), lambda i: (i, 0)),
        ],
        out_shape=[jax.ShapeDtypeStruct((t, na), F32), jax.ShapeDtypeStruct((t, nb), F32)],
        compiler_params=_params(("parallel",)),
        name="in_proj",
    )(x2, g, wa, wb)


def _post_kernel(x_ref, a_ref, b_ref, wa_ref, wb_ref, g_ref, wg_ref, wu_ref, wd_ref, gf_ref, o_ref, *, final):
    x = x_ref[...]
    x = x + jnp.dot(a_ref[...].astype(BF16), wa_ref[...], preferred_element_type=F32)
    x = x + jnp.dot(b_ref[...].astype(BF16), wb_ref[...], preferred_element_type=F32)
    h = _rms(x, g_ref[...]).astype(BF16)
    gate = jnp.dot(h, wg_ref[...], preferred_element_type=F32)
    up = jnp.dot(h, wu_ref[...], preferred_element_type=F32)
    act = (gate * _sigmoid(gate)) * up
    x = x + jnp.dot(act.astype(BF16), wd_ref[...], preferred_element_type=F32)
    if final:
        x = _rms(x, gf_ref[...])
    o_ref[...] = x


def _post(x2, a2, b2, wa, wb, g, wg, wu, wd, gf, tm, final):
    t = x2.shape[0]
    ka, kb, dff = a2.shape[1], b2.shape[1], wg.shape[1]
    return pl.pallas_call(
        functools.partial(_post_kernel, final=final),
        grid=(t // tm,),
        in_specs=[
            pl.BlockSpec((tm, D_MODEL), lambda i: (i, 0)),
            pl.BlockSpec((tm, ka), lambda i: (i, 0)),
            pl.BlockSpec((tm, kb), lambda i: (i, 0)),
            _const_spec((ka, D_MODEL)),
            _const_spec((kb, D_MODEL)),
            _const_spec((1, D_MODEL)),
            _const_spec((D_MODEL, dff)),
            _const_spec((D_MODEL, dff)),
            _const_spec((dff, D_MODEL)),
            _const_spec((1, D_MODEL)),
        ],
        out_specs=pl.BlockSpec((tm, D_MODEL), lambda i: (i, 0)),
        out_shape=jax.ShapeDtypeStruct((t, D_MODEL), F32),
        compiler_params=_params(("parallel",)),
        name="post_final" if final else "post",
    )(x2, a2, b2, wa, wb, g, wg, wu, wd, gf)


def _ret_kernel(cdec_ref, q_ref, k_ref, v_ref, gt_ref, cos_ref, sin_ref, dmat_ref, qdec_ref, kdec_ref, gn_ref, r0_ref,
                o_ref, rfin_ref, r_scr):
    j = pl.program_id(1)

    @pl.when(j == 0)
    def _():
        r_scr[...] = r0_ref[...]

    cos = cos_ref[...]
    sin = sin_ref[...]
    for h in range(RET_HEADS):
        sl = slice(h * RET_DK, (h + 1) * RET_DK)
        q = q_ref[:, sl]
        k = k_ref[:, sl]
        v = v_ref[:, sl]
        qr = q * cos + pltpu.roll(q, RET_DK // 2, 1) * sin
        kr = (k * cos + pltpu.roll(k, RET_DK // 2, 1) * sin) * (RET_DK ** -0.5)
        inner = _mm(qr, kr, NT) * dmat_ref[h]
        r = r_scr[h]
        o = _mm(inner, v) + _mm(qr, r) * qdec_ref[h]
        r_scr[h] = r * cdec_ref[h] + _mm(kr * kdec_ref[h], v, TN)
        oc = o - jnp.mean(o, axis=-1, keepdims=True)
        y = oc * lax.rsqrt(jnp.mean(oc * oc, axis=-1, keepdims=True) + LN_EPS) * gn_ref[:, sl]
        gt = gt_ref[:, sl]
        o_ref[:, sl] = (gt * _sigmoid(gt)) * y

    @pl.when(j == pl.num_programs(1) - 1)
    def _():
        rfin_ref[...] = r_scr[...]


def _ret_tables(c):
    lg = np.log1p(-np.exp2(-5.0 - np.arange(RET_HEADS, dtype=np.float64)))
    idx = np.arange(c, dtype=np.float64)
    diff = idx[:, None] - idx[None, :]
    dmat = np.where(diff >= 0, np.exp(lg[:, None, None] * np.maximum(diff, 0.0)), 0.0)
    qdec = np.exp(lg[:, None] * (idx + 1.0))
    kdec = np.exp(lg[:, None] * (c - 1.0 - idx))
    qdec = np.broadcast_to(qdec[:, :, None], (RET_HEADS, c, RET_DV))
    kdec = np.broadcast_to(kdec[:, :, None], (RET_HEADS, c, RET_DK))
    cdec = np.exp(lg * c)
    return tuple(jnp.asarray(t, F32) for t in (dmat, qdec, kdec, cdec))


def _retention(qkvg, cos, sin, gn, r0, c):
    bsz, seq, _ = qkvg.shape
    dmat, qdec, kdec, cdec = _ret_tables(c)
    col = lambda n: pl.BlockSpec((None, c, RET_WIDTH), lambda b, j, n=n: (b, j, n))
    return pl.pallas_call(
        _ret_kernel,
        grid=(bsz, seq // c),
        in_specs=[
            pl.BlockSpec(memory_space=pltpu.SMEM),
            col(0), col(1), col(2), col(3),
            pl.BlockSpec((c, RET_DK), lambda b, j: (j, 0)),
            pl.BlockSpec((c, RET_DK), lambda b, j: (j, 0)),
            _const_spec((RET_HEADS, c, c)),
            _const_spec((RET_HEADS, c, RET_DV)),
            _const_spec((RET_HEADS, c, RET_DK)),
            _const_spec((1, RET_WIDTH)),
            pl.BlockSpec((None, RET_HEADS, RET_DK, RET_DV), lambda b, j: (b, 0, 0, 0)),
        ],
        out_specs=[
            pl.BlockSpec((None, c, RET_WIDTH), lambda b, j: (b, j, 0)),
            pl.BlockSpec((None, RET_HEADS, RET_DK, RET_DV), lambda b, j: (b, 0, 0, 0)),
        ],
        out_shape=[
            jax.ShapeDtypeStruct((bsz, seq, RET_WIDTH), F32),
            jax.ShapeDtypeStruct((bsz, RET_HEADS, RET_DK, RET_DV), F32),
        ],
        scratch_shapes=[pltpu.VMEM((RET_HEADS, RET_DK, RET_DV), F32)],
        compiler_params=_params(("parallel", "arbitrary")),
        name="retention",
    )(cdec, qkvg, qkvg, qkvg, qkvg, cos, sin, dmat, qdec, kdec, gn, r0)


GROUP_LANES = 256
GROUP_HEADS = GROUP_LANES // RWKV_HD


def _rwkv_kernel(p_ref, prev0_ref, h0_ref, mu_ref, vec_ref, wwa_ref, g2_ref, mseg_ref, ltri_ref,
                 o_ref, hfin_ref, h_scr, prev_scr):
    j = pl.program_id(1)
    c = CHUNK
    w = RWKV_WIDTH

    @pl.when(j == 0)
    def _():
        h_scr[...] = h0_ref[...]
        prev_scr[...] = prev0_ref[...]

    p = p_ref[...]
    row = lax.broadcasted_iota(jnp.int32, p.shape, 0)
    p_prev = jnp.where(row == 0, prev_scr[...], pltpu.roll(p, 1, 0))
    prev_scr[...] = p[c - 1:c, :]
    xs = p + (p_prev - p) * mu_ref[...]

    r = xs[:, 0:w]
    k = xs[:, w:2 * w]
    v = xs[:, 2 * w:3 * w]
    wa_in = xs[:, 3 * w:3 * w + 128]
    lane = lax.broadcasted_iota(jnp.int32, wa_in.shape, 1)
    wa_in = jnp.where(lane < RWKV_W_LORA, jnp.tanh(wa_in), wa_in)
    wa = _mm(wa_in, wwa_ref[...])
    g = _mm(_sigmoid(xs[:, 3 * w + 128:]), g2_ref[...])

    w0 = vec_ref[0:1, :]
    a0 = vec_ref[1:2, :]
    k_k = vec_ref[2:3, :]
    k_a = vec_ref[3:4, :]
    r_k = vec_ref[4:5, :]
    ln_g = vec_ref[5:6, :]
    ln_b = vec_ref[6:7, :]
    mseg = mseg_ref[...]

    def segsum(t):
        return _mm(t, mseg, NN, na=3, nb=1)

    z = -(w0 + wa[:, :w])
    softplus = jnp.maximum(z, 0.0) + jnp.log1p(jnp.exp(-jnp.abs(z)))
    lw = -jnp.exp(-softplus - 0.5)
    a = _sigmoid(a0 + wa[:, w:])
    kk = k * k_k
    kkn = kk / jnp.maximum(jnp.sqrt(segsum(kk * kk)), 1e-12)
    k2 = k * (1.0 + (a - 1.0) * k_a)
    bonus = segsum(r * k2 * r_k) * v

    cum = _mm(ltri_ref[...], lw, NN, na=1, nb=3)
    cum_last = cum[c - 1:c, :]
    e_pos = jnp.exp(cum)
    e_neg = jnp.exp(-cum)
    e_rel = jnp.exp(cum_last - cum)
    b = kkn * a
    a_hat = -kkn * jnp.exp(cum - lw)
    r_hat = r * e_pos
    b_hat = b * e_neg
    k_hat = k2 * e_neg
    b_til = b * e_rel
    k_til = k2 * e_rel
    w_c = jnp.exp(cum_last)

    gl = GROUP_LANES
    shift = jnp.int32(RWKV_HD.bit_length() - 1)
    rb = lax.shift_right_logical(lax.broadcasted_iota(jnp.int32, (gl, gl), 0), shift)
    cb = lax.shift_right_logical(lax.broadcasted_iota(jnp.int32, (gl, gl), 1), shift)
    diag_mask = rb == cb
    trow = lax.broadcasted_iota(jnp.int32, (c, gl), 0)
    tcol = jnp.bitwise_and(lax.broadcasted_iota(jnp.int32, (c, gl), 1), jnp.int32(RWKV_HD - 1))
    strict = trow > tcol
    incl = trow >= tcol
    eye = jnp.where(trow == tcol, 1.0, 0.0).astype(F32)
    ones = jnp.ones((c, gl), F32)

    def bd(y):
        return jnp.where(diag_mask, jnp.concatenate([y] * GROUP_HEADS, axis=0), 0.0)

    def unbd(full):
        m = jnp.where(diag_mask, full, 0.0)
        out = m[0:RWKV_HD]
        for hh in range(1, GROUP_HEADS):
            out = out + m[hh * RWKV_HD:(hh + 1) * RWKV_HD]
        return out

    def mm3(x, y, dn=NN):
        return _mm(x, y, dn, na=2, nb=2)

    ys = []
    for gi in range(w // gl):
        sl = slice(gi * gl, (gi + 1) * gl)
        ah, rh, bh, kh = a_hat[:, sl], r_hat[:, sl], b_hat[:, sl], k_hat[:, sl]
        bt, kt, vg = b_til[:, sl], k_til[:, sl], v[:, sl]
        lhs = jnp.concatenate([ah, rh], axis=0)
        p1 = mm3(lhs, bd(bh), NT)
        p2 = mm3(lhs, bd(kh), NT)
        a_ab = jnp.where(strict, p1[:c], 0.0)
        p_rb = jnp.where(incl, p1[c:], 0.0)
        a_ak = jnp.where(strict, p2[:c], 0.0)
        p_rk = jnp.where(incl, p2[c:], 0.0)

        t_inv = eye + a_ab
        pw = mm3(a_ab, bd(a_ab))
        for _ in range(4):
            tp = mm3(jnp.concatenate([t_inv, pw], axis=0), bd(pw))
            t_inv = t_inv + tp[:c]
            pw = tp[c:]
        t_inv = t_inv + mm3(t_inv, bd(pw))

        av = mm3(jnp.concatenate([a_ak, p_rk], axis=0), bd(vg))
        a_til = mm3(t_inv, bd(ah))
        v_til = mm3(t_inv, bd(av[:c]))
        kv = unbd(mm3(kt, vg, TN))
        wc_col = unbd(_mm(eye * w_c[:, sl], ones, TN, na=3, nb=1))

        h_old = h_scr[:, sl]
        ur = mm3(jnp.concatenate([a_til, rh], axis=0), bd(h_old))
        u = ur[:c] + v_til
        ys.append(ur[c:] + mm3(p_rb, bd(u)) + av[c:])
        h_scr[:, sl] = wc_col * h_old + unbd(mm3(bt, u, TN)) + kv

    y = jnp.concatenate(ys, axis=1)
    mean = segsum(y) * (1.0 / RWKV_HD)
    yc = y - mean
    var = segsum(yc * yc) * (1.0 / RWKV_HD)
    yn = yc * lax.rsqrt(var + RWKV_LN_EPS) * ln_g + ln_b
    o_ref[...] = (yn + bonus) * g

    @pl.when(j == pl.num_programs(1) - 1)
    def _():
        hfin_ref[...] = h_scr[...]


def _rwkv(p, prev0, h0, mu, vecs, wwa, g2, mseg, ltri):
    bsz, seq, _ = p.shape
    c = CHUNK
    return pl.pallas_call(
        _rwkv_kernel,
        grid=(bsz, seq // c),
        in_specs=[
            pl.BlockSpec((None, c, RWKV_PROJ), lambda b, j: (b, j, 0)),
            pl.BlockSpec((None, 1, RWKV_PROJ), lambda b, j: (b, 0, 0)),
            pl.BlockSpec((None, RWKV_HD, RWKV_WIDTH), lambda b, j: (b, 0, 0)),
            _const_spec((1, RWKV_PROJ)),
            _const_spec((8, RWKV_WIDTH)),
            _const_spec((128, 2 * RWKV_WIDTH)),
            _const_spec((RWKV_G_LORA, RWKV_WIDTH)),
            _const_spec((RWKV_WIDTH, RWKV_WIDTH)),
            _const_spec((c, c)),
        ],
        out_specs=[
            pl.BlockSpec((None, c, RWKV_WIDTH), lambda b, j: (b, j, 0)),
            pl.BlockSpec((None, RWKV_HD, RWKV_WIDTH), lambda b, j: (b, 0, 0)),
        ],
        out_shape=[
            jax.ShapeDtypeStruct((bsz, seq, RWKV_WIDTH), F32),
            jax.ShapeDtypeStruct((bsz, RWKV_HD, RWKV_WIDTH), F32),
        ],
        scratch_shapes=[pltpu.VMEM((RWKV_HD, RWKV_WIDTH), F32), pltpu.VMEM((1, RWKV_PROJ), F32)],
        compiler_params=_params(("parallel", "arbitrary")),
        name="rwkv7",
    )(p, prev0, h0, mu, vecs, wwa, g2, mseg, ltri)


def _conv_kernel(cu_ref, buf_ref, w_ref, vec_ref, o_ref, bufout_ref, hist, *, tm):
    j = pl.program_id(1)

    @pl.when(j == 0)
    def _():
        hist[0:CONV_PAD, :] = buf_ref[...]

    cu = cu_ref[...]
    hist[CONV_PAD:CONV_PAD + tm, :] = cu[:, :CONV_CH] * _sigmoid(cu[:, CONV_CH:])
    off = CONV_PAD - (CONV_WIDTH - 1)
    acc = jnp.broadcast_to(vec_ref[0:1, :], (tm, CONV_CH))
    for i in range(CONV_WIDTH):
        acc = acc + hist[off + i:off + i + tm, :] * w_ref[i:i + 1, :]
    yc = acc - jnp.mean(acc, axis=-1, keepdims=True)
    y = yc * lax.rsqrt(jnp.mean(yc * yc, axis=-1, keepdims=True) + LN_EPS) * vec_ref[1:2, :] + vec_ref[2:3, :]
    o_ref[...] = y * _sigmoid(y)
    tail = hist[tm:tm + CONV_PAD, :]
    hist[0:CONV_PAD, :] = tail

    @pl.when(j == pl.num_programs(1) - 1)
    def _():
        bufout_ref[...] = tail[off:, :]


def _conv(cu, buf_pad, conv_w, vecs, tm):
    bsz, seq, _ = cu.shape
    return pl.pallas_call(
        functools.partial(_conv_kernel, tm=tm),
        grid=(bsz, seq // tm),
        in_specs=[
            pl.BlockSpec((None, tm, 2 * CONV_CH), lambda b, j: (b, j, 0)),
            pl.BlockSpec((None, CONV_PAD, CONV_CH), lambda b, j: (b, 0, 0)),
            _const_spec((CONV_PAD, CONV_CH)),
            _const_spec((8, CONV_CH)),
        ],
        out_specs=[
            pl.BlockSpec((None, tm, CONV_CH), lambda b, j: (b, j, 0)),
            pl.BlockSpec((None, CONV_WIDTH - 1, CONV_CH), lambda b, j: (b, 0, 0)),
        ],
        out_shape=[
            jax.ShapeDtypeStruct((bsz, seq, CONV_CH), F32),
            jax.ShapeDtypeStruct((bsz, CONV_WIDTH - 1, CONV_CH), F32),
        ],
        scratch_shapes=[pltpu.VMEM((tm + CONV_PAD, CONV_CH), F32)],
        compiler_params=_params(("parallel", "arbitrary")),
        name="conv_module",
    )(cu, buf_pad, conv_w, vecs)


KV_W = ATT_KV_HEADS * ATT_HD
Q_W = ATT_HEADS * ATT_HD


def _rope_small(x, cos_f, sin_a, sin_b):
    n = x.shape[1]
    return x * cos_f + pltpu.roll(x, n - ROT_DIM // 2, 1) * sin_a + pltpu.roll(x, ROT_DIM // 2, 1) * sin_b


def _attn_kernel(sink_ref, q_ref, k_ref, v_ref, cos_ref, sa_ref, sb_ref, pk_ref, pv_ref,
                 o_ref, knew_ref, vnew_ref, kwin, vwin, *, tq, has_past):
    j = pl.program_id(1)
    c = CHUNK

    @pl.when(j == 0)
    def _():
        kwin[0:WINDOW, :] = pk_ref[...]
        vwin[0:WINDOW, :] = pv_ref[...]

    cos_f, sin_a, sin_b = cos_ref[...], sa_ref[...], sb_ref[...]
    kwin[WINDOW:WINDOW + tq, :] = _rope_small(k_ref[...], cos_f, sin_a, sin_b)
    vwin[WINDOW:WINDOW + tq, :] = v_ref[...]
    rep = Q_W // KV_W
    qr = _rope_small(q_ref[...], jnp.concatenate([cos_f] * rep, axis=1), jnp.concatenate([sin_a] * rep, axis=1),
                     jnp.concatenate([sin_b] * rep, axis=1))
    scol = lax.broadcasted_iota(jnp.int32, (c, WINDOW + c), 1)
    for i in range(tq // c):
        keys = kwin[i * c:i * c + WINDOW + c, :]
        vals = vwin[i * c:i * c + WINDOW + c, :]
        outs = []
        for h in range(ATT_HEADS):
            kvh = h // ATT_GROUP
            qh = qr[i * c:(i + 1) * c, h * ATT_HD:(h + 1) * ATT_HD]
            s = _mm(qh, keys[:, kvh * ATT_HD:(kvh + 1) * ATT_HD], NT) * (ATT_HD ** -0.5)
            if not has_past:
                blk = j * (tq // c) + i
                s = jnp.where(blk * c + scol < WINDOW, NEG_INF, s)
            sk = sink_ref[h]
            m = jnp.maximum(jnp.max(s, axis=-1, keepdims=True), sk)
            e = jnp.exp(s - m)
            pr = e / (jnp.sum(e, axis=-1, keepdims=True) + jnp.exp(sk - m))
            outs.append(_mm(pr, vals[:, kvh * ATT_HD:(kvh + 1) * ATT_HD]))
        o_ref[i * c:(i + 1) * c, :] = jnp.concatenate(outs, axis=1)
    ktail = kwin[tq:tq + WINDOW, :]
    vtail = vwin[tq:tq + WINDOW, :]
    kwin[0:WINDOW, :] = ktail
    vwin[0:WINDOW, :] = vtail

    @pl.when(j == pl.num_programs(1) - 1)
    def _():
        knew_ref[...] = ktail
        vnew_ref[...] = vtail


def _attention(qkv, cos_f, sin_a, sin_b, past_k, past_v, sink, tq, has_past):
    bsz, seq, _ = qkv.shape
    nq = Q_W // KV_W
    tab = pl.BlockSpec((tq, KV_W), lambda b, j: (j, 0))
    past = pl.BlockSpec((None, WINDOW, KV_W), lambda b, j: (b, 0, 0))
    return pl.pallas_call(
        functools.partial(_attn_kernel, tq=tq, has_past=has_past),
        grid=(bsz, seq // tq),
        in_specs=[
            pl.BlockSpec(memory_space=pltpu.SMEM),
            pl.BlockSpec((None, tq, Q_W), lambda b, j: (b, j, 0)),
            pl.BlockSpec((None, tq, KV_W), lambda b, j: (b, j, nq)),
            pl.BlockSpec((None, tq, KV_W), lambda b, j: (b, j, nq + 1)),
            tab, tab, tab, past, past,
        ],
        out_specs=[pl.BlockSpec((None, tq, Q_W), lambda b, j: (b, j, 0)), past, past],
        out_shape=[
            jax.ShapeDtypeStruct((bsz, seq, Q_W), F32),
            jax.ShapeDtypeStruct((bsz, WINDOW, KV_W), F32),
            jax.ShapeDtypeStruct((bsz, WINDOW, KV_W), F32),
        ],
        scratch_shapes=[pltpu.VMEM((WINDOW + tq, KV_W), F32), pltpu.VMEM((WINDOW + tq, KV_W), F32)],
        compiler_params=_params(("parallel", "arbitrary")),
        name="swa_attention",
    )(sink, qkv, qkv, qkv, cos_f, sin_a, sin_b, past_k, past_v)


def _rope_tables(pos0, seq, rot_dim, theta):
    half = rot_dim // 2
    inv = np.exp(np.arange(half, dtype=np.float64) * (-2.0 * math.log(theta) / rot_dim))
    ang = np.arange(pos0, pos0 + seq).astype(np.float64)[:, None] * inv[None, :]
    return jnp.asarray(np.cos(ang), F32), jnp.asarray(np.sin(ang), F32)


def _largest_tile(n, cap):
    t = cap
    while n % t:
        t //= 2
    return t


def _prepare(prm):
    w = RWKV_WIDTH
    qkvg_w = 2 * RET_HEADS * RET_DK + 2 * RET_WIDTH
    cu_w = 2 * CONV_CH
    pp = {}
    pp["w_in_ab_a"] = prm["w_in_ab"][0][:, :qkvg_w].astype(BF16)
    pp["w_in_ab_b"] = prm["w_in_ab"][0][:, qkvg_w:].astype(BF16)
    pp["w_out_ab_a"] = prm["w_out_ab"][0][:RET_WIDTH].astype(BF16)
    pp["w_out_ab_b"] = prm["w_out_ab"][0][RET_WIDTH:].astype(BF16)
    pp["w_in_cd_a"] = prm["w_in_cd"][0][:, :cu_w].astype(BF16)
    pp["w_in_cd_b"] = prm["w_in_cd"][0][:, cu_w:].astype(BF16)
    pp["w_out_cd_a"] = prm["w_out_cd"][0][:CONV_CH].astype(BF16)
    pp["w_out_cd_b"] = prm["w_out_cd"][0][CONV_CH:].astype(BF16)
    for name in ("ffn_gate", "ffn_up", "ffn_down"):
        pp[name] = prm[name].astype(BF16)
    zeros = jnp.zeros((RWKV_W_LORA, w), F32)
    pp["rwkv_wwa"] = jnp.concatenate(
        [jnp.concatenate([prm["rwkv_w2"][0], zeros], axis=1), jnp.concatenate([zeros, prm["rwkv_a2"][0]], axis=1)],
        axis=0).astype(BF16)
    pp["rwkv_g2"] = prm["rwkv_g2"][0].astype(BF16)
    pp["rwkv_vecs"] = jnp.stack(
        [prm["rwkv_w0"][0], prm["rwkv_a0"][0], prm["rwkv_kk"][0], prm["rwkv_ka"][0], prm["rwkv_rk"][0].reshape(w),
         prm["rwkv_ln_g"][0], prm["rwkv_ln_b"][0], jnp.zeros((w,), F32)], axis=0)
    pp["rwkv_mu"] = prm["rwkv_mu"][0].reshape(1, RWKV_PROJ)
    seg = jnp.arange(w) // RWKV_HD
    pp["mseg"] = (seg[:, None] == seg[None, :]).astype(BF16)
    t = jnp.arange(CHUNK)
    pp["ltri"] = (t[:, None] >= t[None, :]).astype(BF16)
    pp["conv_w"] = jnp.concatenate([prm["conv_w"][0], jnp.zeros((CONV_PAD - CONV_WIDTH, CONV_CH), F32)], axis=0)
    pp["conv_vecs"] = jnp.concatenate(
        [jnp.stack([prm["conv_b"][0], prm["conv_ln_g"][0], prm["conv_ln_b"][0]], axis=0),
         jnp.zeros((5, CONV_CH), F32)], axis=0)
    return pp


def _trunk(x, pos0, has_past, st_ret, st_wkv, st_shift, st_conv, st_k, st_v, prm, pp):
    bsz, seq, _ = x.shape
    tokens = bsz * seq
    tm = _largest_tile(tokens, 256)
    row = lambda v: v.reshape(1, -1)

    x2 = x.reshape(tokens, D_MODEL)
    qkvg, p = _in_proj(x2, row(prm["norm_mix"][0]), pp["w_in_ab_a"], pp["w_in_ab_b"], tm)
    qkvg = qkvg.reshape(bsz, seq, -1)
    p = p.reshape(bsz, seq, RWKV_PROJ)
    cos, sin = _rope_tables(pos0, seq, RET_DK, RET_THETA)
    cos_t = jnp.concatenate([cos, cos], axis=1)
    sin_t = jnp.concatenate([-sin, sin], axis=1)
    o_ret, ret_new = _retention(qkvg, cos_t, sin_t, row(prm["ret_gn_g"][0]), st_ret, _largest_tile(seq, 256))
    h0 = st_wkv.transpose(0, 3, 1, 2).reshape(bsz, RWKV_HD, RWKV_WIDTH)
    o_rwkv, h_new = _rwkv(p, st_shift.reshape(bsz, 1, RWKV_PROJ), h0, pp["rwkv_mu"], pp["rwkv_vecs"],
                          pp["rwkv_wwa"], pp["rwkv_g2"], pp["mseg"], pp["ltri"])
    wkv_new = h_new.reshape(bsz, RWKV_HD, RWKV_HEADS, RWKV_HD).transpose(0, 2, 3, 1)
    shift_new = p[:, -1]
    x2 = _post(x2, o_ret.reshape(tokens, -1), o_rwkv.reshape(tokens, -1), pp["w_out_ab_a"], pp["w_out_ab_b"],
               row(prm["norm_ffn"][0]), pp["ffn_gate"][0], pp["ffn_up"][0], pp["ffn_down"][0],
               row(prm["norm_final"]), tm, final=False)

    cu, qkv = _in_proj(x2, row(prm["norm_mix"][1]), pp["w_in_cd_a"], pp["w_in_cd_b"], tm)
    cu = cu.reshape(bsz, seq, -1)
    qkv = qkv.reshape(bsz, seq, -1)
    buf_pad = jnp.concatenate([jnp.zeros((bsz, CONV_PAD - (CONV_WIDTH - 1), CONV_CH), F32), st_conv], axis=1)
    o_conv, conv_new = _conv(cu, buf_pad, pp["conv_w"], pp["conv_vecs"], _largest_tile(seq, 256))
    cos, sin = _rope_tables(pos0, seq, ROT_DIM, ROPE_THETA)
    half = ROT_DIM // 2
    pad = jnp.zeros((seq, ATT_HD - ROT_DIM), F32)
    z8 = jnp.zeros((seq, half), F32)
    cos_f = jnp.concatenate([cos, cos, pad + 1.0], axis=1)
    sin_a = jnp.concatenate([-sin, z8, pad], axis=1)
    sin_b = jnp.concatenate([z8, sin, pad], axis=1)
    tile2 = lambda t: jnp.concatenate([t] * ATT_KV_HEADS, axis=1)
    o_att, k_new, v_new = _attention(qkv, tile2(cos_f), tile2(sin_a), tile2(sin_b),
                                     st_k.reshape(bsz, WINDOW, KV_W), st_v.reshape(bsz, WINDOW, KV_W),
                                     prm["attn_sink"][0], _largest_tile(seq, 256), has_past)
    y2 = _post(x2, o_conv.reshape(tokens, -1), o_att.reshape(tokens, -1), pp["w_out_cd_a"], pp["w_out_cd_b"],
               row(prm["norm_ffn"][1]), pp["ffn_gate"][1], pp["ffn_up"][1], pp["ffn_down"][1],
               row(prm["norm_final"]), tm, final=True)
    kv_shape = (bsz, WINDOW, ATT_KV_HEADS, ATT_HD)
    return (y2.reshape(bsz, seq, D_MODEL), ret_new[None], wkv_new[None], shift_new[None], conv_new[None],
            k_new.reshape(kv_shape)[None], v_new.reshape(kv_shape)[None])


def kernel(x_prompt, x_sample, state_ret, state_wkv, state_shift, state_conv, cache_k, cache_v, norm_mix, norm_ffn, norm_final, w_in_ab, w_out_ab, ret_gn_g, rwkv_mu, rwkv_w0, rwkv_w2, rwkv_a0, rwkv_a2, rwkv_g2, rwkv_kk, rwkv_ka, rwkv_rk, rwkv_ln_g, rwkv_ln_b, w_in_cd, w_out_cd, conv_w, conv_b, conv_ln_g, conv_ln_b, attn_sink, ffn_gate, ffn_up, ffn_down):
    prm = {
        'norm_mix': norm_mix, 'norm_ffn': norm_ffn, 'norm_final': norm_final,
        'w_in_ab': w_in_ab, 'w_out_ab': w_out_ab, 'ret_gn_g': ret_gn_g,
        'rwkv_mu': rwkv_mu, 'rwkv_w0': rwkv_w0, 'rwkv_w2': rwkv_w2, 'rwkv_a0': rwkv_a0, 'rwkv_a2': rwkv_a2,
        'rwkv_g2': rwkv_g2, 'rwkv_kk': rwkv_kk, 'rwkv_ka': rwkv_ka, 'rwkv_rk': rwkv_rk,
        'rwkv_ln_g': rwkv_ln_g, 'rwkv_ln_b': rwkv_ln_b,
        'w_in_cd': w_in_cd, 'w_out_cd': w_out_cd, 'conv_w': conv_w, 'conv_b': conv_b,
        'conv_ln_g': conv_ln_g, 'conv_ln_b': conv_ln_b, 'attn_sink': attn_sink,
        'ffn_gate': ffn_gate, 'ffn_up': ffn_up, 'ffn_down': ffn_down,
    }
    pp = _prepare(prm)
    bsz = x_prompt.shape[0]
    dt = x_prompt.dtype
    past_len = 4096
    z_ret = jnp.zeros((bsz, RET_HEADS, RET_DK, RET_DV), dt)
    z_wkv = jnp.zeros((bsz, RWKV_HEADS, RWKV_HD, RWKV_HD), dt)
    z_shift = jnp.zeros((bsz, RWKV_PROJ), dt)
    z_conv = jnp.zeros((bsz, CONV_WIDTH - 1, CONV_CH), dt)
    z_kv = jnp.zeros((bsz, WINDOW, ATT_KV_HEADS, ATT_HD), dt)
    out_p = _trunk(x_prompt, 0, False, z_ret, z_wkv, z_shift, z_conv, z_kv, z_kv, prm, pp)
    out_s = _trunk(x_sample, past_len, True, state_ret[0], state_wkv[0], state_shift[0], state_conv[0],
                   cache_k[0], cache_v[0], prm, pp)
    return (out_p[0], out_s[0]) + tuple(out_p[1:]) + tuple(out_s[1:])
```

```python
import functools
import math

import jax
import jax.numpy as jnp
import numpy as np
from jax import lax
from jax.experimental import pallas as pl
from jax.experimental.pallas import tpu as pltpu

F32 = jnp.float32
BF16 = jnp.bfloat16

D_MODEL = 1024
CHUNK = 64
RMS_EPS = 1e-6
LN_EPS = 1e-5

RET_HEADS = 4
RET_DK = 128
RET_DV = 128
RET_WIDTH = RET_HEADS * RET_DV
RET_THETA = 10000.0

RWKV_HEADS = 8
RWKV_HD = 64
RWKV_WIDTH = RWKV_HEADS * RWKV_HD
RWKV_W_LORA = 64
RWKV_A_LORA = 64
RWKV_WA_LORA = RWKV_W_LORA + RWKV_A_LORA
RWKV_G_LORA = 128
RWKV_PROJ = 3 * RWKV_WIDTH + RWKV_WA_LORA + RWKV_G_LORA
RWKV_LN_EPS = 64e-5

CONV_CH = 512
CONV_WIDTH = 31
CONV_PAD = 32
SUBLANES = 8

ATT_HEADS = 8
ATT_KV_HEADS = 2
ATT_HD = 64
ATT_GROUP = ATT_HEADS // ATT_KV_HEADS
WINDOW = 128
ROT_DIM = ATT_HD // 4
ROPE_THETA = 500000.0
NEG_INF = -1e30
LANES = 128

PAST_LEN = 4096

V7X_VMEM_LIMIT_BYTES = 56 * 1024 * 1024

NN = (((1,), (0,)), ((), ()))
NT = (((1,), (1,)), ((), ()))
TN = (((0,), (0,)), ((), ()))


def _split_bf16(x, n):
    pieces = []
    rem = x
    for i in range(n):
        piece = rem.astype(BF16)
        pieces.append(piece)
        if i + 1 < n:
            rem = rem - piece.astype(F32)
    return pieces


def _mmp(pa, pb, dn=NN):
    n = max(len(pa), len(pb))
    acc = None
    for i, x in enumerate(pa):
        for j, y in enumerate(pb):
            if i + j < n:
                t = lax.dot_general(x, y, dn, preferred_element_type=F32)
                acc = t if acc is None else acc + t
    return acc


def _mm(a, b, dn=NN, na=1, nb=1):
    return _mmp(_split_bf16(a, na), _split_bf16(b, nb), dn)


def _rms(x, g):
    ms = jnp.mean(x * x, axis=-1, keepdims=True)
    return (x * lax.rsqrt(ms + RMS_EPS)) * g


def _sigmoid(x):
    return 1.0 / (1.0 + jnp.exp(-x))


def _params(sem):
    return pltpu.CompilerParams(dimension_semantics=sem, vmem_limit_bytes=V7X_VMEM_LIMIT_BYTES)


def _const_spec(shape):
    nd = len(shape)
    return pl.BlockSpec(shape, lambda *_: (0,) * nd, pipeline_mode=pl.Buffered(1))


def _in_proj_kernel(x_ref, g_ref, wa_ref, wb_ref, oa_ref, ob_ref):
    h = _rms(x_ref[...], g_ref[...]).astype(BF16)
    oa_ref[...] = jnp.dot(h, wa_ref[...], preferred_element_type=F32)
    ob_ref[...] = jnp.dot(h, wb_ref[...], preferred_element_type=F32)


def _in_proj(x2, g, wa, wb, tm):
    t = x2.shape[0]
    na, nb = wa.shape[1], wb.shape[1]
    return pl.pallas_call(
        _in_proj_kernel,
        grid=(t // tm,),
        in_specs=[
            pl.BlockSpec((tm, D_MODEL), lambda i: (i, 0)),
            _const_spec((1, D_MODEL)),
            _const_spec((D_MODEL, na)),
            _const_spec((D_MODEL, nb)),
        ],
        out_specs=[
            pl.BlockSpec((tm, na), lambda i: (i, 0)),
            pl.BlockSpec((tm, nb), lambda i: (i, 0)),
        ],
        out_shape=[jax.ShapeDtypeStruct((t, na), F32), jax.ShapeDtypeStruct((t, nb), F32)],
        compiler_params=_params(("parallel",)),
        name="in_proj",
    )(x2, g, wa, wb)


def _post_kernel(x_ref, a_ref, b_ref, wa_ref, wb_ref, g_ref, wg_ref, wu_ref, wd_ref, gf_ref, o_ref, *, final):
    x = x_ref[...]
    x = x + jnp.dot(a_ref[...].astype(BF16), wa_ref[...], preferred_element_type=F32)
    x = x + jnp.dot(b_ref[...].astype(BF16), wb_ref[...], preferred_element_type=F32)
    h = _rms(x, g_ref[...]).astype(BF16)
    gate = jnp.dot(h, wg_ref[...], preferred_element_type=F32)
    up = jnp.dot(h, wu_ref[...], preferred_element_type=F32)
    act = (gate * _sigmoid(gate)) * up
    x = x + jnp.dot(act.astype(BF16), wd_ref[...], preferred_element_type=F32)
    if final:
        x = _rms(x, gf_ref[...])
    o_ref[...] = x


def _post(x2, a2, b2, wa, wb, g, wg, wu, wd, gf, tm, final):
    t = x2.shape[0]
    ka, kb, dff = a2.shape[1], b2.shape[1], wg.shape[1]
    return pl.pallas_call(
        functools.partial(_post_kernel, final=final),
        grid=(t // tm,),
        in_specs=[
            pl.BlockSpec((tm, D_MODEL), lambda i: (i, 0)),
            pl.BlockSpec((tm, ka), lambda i: (i, 0)),
            pl.BlockSpec((tm, kb), lambda i: (i, 0)),
            _const_spec((ka, D_MODEL)),
            _const_spec((kb, D_MODEL)),
            _const_spec((1, D_MODEL)),
            _const_spec((D_MODEL, dff)),
            _const_spec((D_MODEL, dff)),
            _const_spec((dff, D_MODEL)),
            _const_spec((1, D_MODEL)),
        ],
        out_specs=pl.BlockSpec((tm, D_MODEL), lambda i: (i, 0)),
        out_shape=jax.ShapeDtypeStruct((t, D_MODEL), F32),
        compiler_params=_params(("parallel",)),
        name="post_final" if final else "post",
    )(x2, a2, b2, wa, wb, g, wg, wu, wd, gf)


def _ret_kernel(cdec_ref, q_ref, k_ref, v_ref, gt_ref, cos_ref, sin_ref, dmat_ref, qdec_ref, kdec_ref, gn_ref, r0_ref,
                o_ref, rfin_ref, r_scr):
    j = pl.program_id(1)

    @pl.when(j == 0)
    def _():
        r_scr[...] = r0_ref[...]

    cos = cos_ref[...]
    sin = sin_ref[...]
    for h in range(RET_HEADS):
        sl = slice(h * RET_DK, (h + 1) * RET_DK)
        q = q_ref[:, sl]
        k = k_ref[:, sl]
        v = v_ref[:, sl]
        qr = q * cos + pltpu.roll(q, RET_DK // 2, 1) * sin
        kr = (k * cos + pltpu.roll(k, RET_DK // 2, 1) * sin) * (RET_DK ** -0.5)
        inner = _mm(qr, kr, NT) * dmat_ref[h]
        r = r_scr[h]
        o = _mm(inner, v) + _mm(qr, r) * qdec_ref[h]
        r_scr[h] = r * cdec_ref[h] + _mm(kr * kdec_ref[h], v, TN)
        oc = o - jnp.mean(o, axis=-1, keepdims=True)
        y = oc * lax.rsqrt(jnp.mean(oc * oc, axis=-1, keepdims=True) + LN_EPS) * gn_ref[:, sl]
        gt = gt_ref[:, sl]
        o_ref[:, sl] = (gt * _sigmoid(gt)) * y

    @pl.when(j == pl.num_programs(1) - 1)
    def _():
        rfin_ref[...] = r_scr[...]


def _ret_tables(c):
    lg = np.log1p(-np.exp2(-5.0 - np.arange(RET_HEADS, dtype=np.float64)))
    idx = np.arange(c, dtype=np.float64)
    diff = idx[:, None] - idx[None, :]
    dmat = np.where(diff >= 0, np.exp(lg[:, None, None] * np.maximum(diff, 0.0)), 0.0)
    qdec = np.exp(lg[:, None] * (idx + 1.0))
    kdec = np.exp(lg[:, None] * (c - 1.0 - idx))
    qdec = np.broadcast_to(qdec[:, :, None], (RET_HEADS, c, RET_DV))
    kdec = np.broadcast_to(kdec[:, :, None], (RET_HEADS, c, RET_DK))
    cdec = np.exp(lg * c)
    return tuple(jnp.asarray(t, F32) for t in (dmat, qdec, kdec, cdec))


def _retention(qkvg, cos, sin, gn, r0, c):
    bsz, seq, _ = qkvg.shape
    dmat, qdec, kdec, cdec = _ret_tables(c)
    col = lambda n: pl.BlockSpec((None, c, RET_WIDTH), lambda b, j, n=n: (b, j, n))
    return pl.pallas_call(
        _ret_kernel,
        grid=(bsz, seq // c),
        in_specs=[
            pl.BlockSpec(memory_space=pltpu.SMEM),
            col(0), col(1), col(2), col(3),
            pl.BlockSpec((c, RET_DK), lambda b, j: (j, 0)),
            pl.BlockSpec((c, RET_DK), lambda b, j: (j, 0)),
            _const_spec((RET_HEADS, c, c)),
            _const_spec((RET_HEADS, c, RET_DV)),
            _const_spec((RET_HEADS, c, RET_DK)),
            _const_spec((1, RET_WIDTH)),
            pl.BlockSpec((None, RET_HEADS, RET_DK, RET_DV), lambda b, j: (b, 0, 0, 0)),
        ],
        out_specs=[
            pl.BlockSpec((None, c, RET_WIDTH), lambda b, j: (b, j, 0)),
            pl.BlockSpec((None, RET_HEADS, RET_DK, RET_DV), lambda b, j: (b, 0, 0, 0)),
        ],
        out_shape=[
            jax.ShapeDtypeStruct((bsz, seq, RET_WIDTH), F32),
            jax.ShapeDtypeStruct((bsz, RET_HEADS, RET_DK, RET_DV), F32),
        ],
        scratch_shapes=[pltpu.VMEM((RET_HEADS, RET_DK, RET_DV), F32)],
        compiler_params=_params(("parallel", "arbitrary")),
        name="retention",
    )(cdec, qkvg, qkvg, qkvg, qkvg, cos, sin, dmat, qdec, kdec, gn, r0)


GROUP_LANES = 256
GROUP_HEADS = GROUP_LANES // RWKV_HD
N_GROUPS = RWKV_WIDTH // GROUP_LANES

RWKV_PREC = {
    "seg": 2,
    "cum": 2,
    "p12": (1, 1),
    "dbl": (1, 1),
    "av": (1, 1),
    "til": (1, 1),
    "kv": (1, 1),
    "wc": 2,
    "ur": (1, 1),
    "yu": (1, 1),
    "hu": (1, 1),
}


def _rwkv_kernel(p_ref, prev0_ref, h0_ref, mu_ref, vec_ref, wwa_ref, g2_ref, ltri_ref, bdm_ref,
                 o_ref, hfin_ref, h_scr, prev_scr, *, bb, tcb):
    j = pl.program_id(1)
    c = CHUNK
    w = RWKV_WIDTH
    gl = GROUP_LANES
    rows = bb * tcb
    prec = RWKV_PREC

    @pl.when(j == 0)
    def _():
        h_scr[...] = h0_ref[...]
        prev_scr[...] = prev0_ref[...]

    p = p_ref[...].reshape(rows, RWKV_PROJ)
    row = lax.broadcasted_iota(jnp.int32, p.shape, 0)
    p_prev = pltpu.roll(p, 1, 0)
    for b in range(bb):
        p_prev = jnp.where(row == b * tcb, prev_scr[b], p_prev)
        prev_scr[b] = p[(b + 1) * tcb - 1:(b + 1) * tcb, :]
    xs = p + (p_prev - p) * mu_ref[...]

    r = xs[:, 0:w]
    k = xs[:, w:2 * w]
    v = xs[:, 2 * w:3 * w]
    wa_in = xs[:, 3 * w:3 * w + RWKV_WA_LORA]
    lane = lax.broadcasted_iota(jnp.int32, wa_in.shape, 1)
    wa_in = jnp.where(lane < RWKV_W_LORA, jnp.tanh(wa_in), wa_in)
    wa = _mm(wa_in, wwa_ref[...])
    g = _mm(_sigmoid(xs[:, 3 * w + RWKV_WA_LORA:]), g2_ref[...])

    w0 = vec_ref[0:1, :]
    a0 = vec_ref[1:2, :]
    k_k = vec_ref[2:3, :]
    k_a = vec_ref[3:4, :]
    r_k = vec_ref[4:5, :]
    ln_g = vec_ref[5:6, :]
    ln_b = vec_ref[6:7, :]
    bdm = bdm_ref[...]

    def segsum(t):
        return jnp.concatenate(
            [_mmp(_split_bf16(t[:, gi * gl:(gi + 1) * gl], prec["seg"]), [bdm]) for gi in range(N_GROUPS)], axis=1)

    z = -(w0 + wa[:, :w])
    softplus = jnp.maximum(z, 0.0) + jnp.log1p(jnp.exp(-jnp.abs(z)))
    lw = -jnp.exp(-softplus - 0.5)
    a = _sigmoid(a0 + wa[:, w:])
    kk = k * k_k
    kkn = kk / jnp.maximum(jnp.sqrt(segsum(kk * kk)), 1e-12)
    k2 = k * (1.0 + (a - 1.0) * k_a)
    bonus = segsum(r * k2 * r_k) * v
    b_gate = kkn * a
    cum = _mmp([ltri_ref[...]], _split_bf16(lw, prec["cum"]))

    shift = jnp.int32(RWKV_HD.bit_length() - 1)
    lane_blk = lax.shift_right_logical(lax.broadcasted_iota(jnp.int32, (c, gl), 1), shift)
    trow = lax.broadcasted_iota(jnp.int32, (c, gl), 0)
    tcol = jnp.bitwise_and(lax.broadcasted_iota(jnp.int32, (c, gl), 1), jnp.int32(RWKV_HD - 1))
    strict = trow > tcol
    incl = trow >= tcol
    eye = jnp.where(trow == tcol, 1.0, 0.0).astype(F32)
    ones_b = jnp.ones((c, gl), BF16)

    def bd(pieces):
        return [jnp.concatenate([q] * GROUP_HEADS, axis=0) * bdm for q in pieces]

    def unbd(full):
        out = full[0:RWKV_HD]
        for hh in range(1, GROUP_HEADS):
            out = jnp.where(lane_blk == hh, full[hh * RWKV_HD:(hh + 1) * RWKV_HD], out)
        return out

    def prod(key, x, y, dn=NN, block_diag=True):
        na, nb = prec[key]
        pb = _split_bf16(y, nb)
        return _mmp(_split_bf16(x, na), bd(pb) if block_diag else pb, dn)

    def chunk_group(ah, rh, bh, kh, bt, kt, vg, wc, h_old):
        lhs = jnp.concatenate([ah, rh], axis=0)
        p1 = prod("p12", lhs, bh, NT)
        p2 = prod("p12", lhs, kh, NT)
        a_ab = jnp.where(strict, p1[:c], 0.0)
        p_rb = jnp.where(incl, p1[c:], 0.0)
        a_ak = jnp.where(strict, p2[:c], 0.0)
        p_rk = jnp.where(incl, p2[c:], 0.0)

        t_inv = eye + a_ab
        pw = prod("dbl", a_ab, a_ab)
        for _ in range(int(math.log2(c)) - 2):
            tp = prod("dbl", jnp.concatenate([t_inv, pw], axis=0), pw)
            t_inv = t_inv + tp[:c]
            pw = tp[c:]
        t_inv = t_inv + prod("dbl", t_inv, pw)

        av = prod("av", jnp.concatenate([a_ak, p_rk], axis=0), vg)
        a_til = prod("til", t_inv, ah)
        v_til = prod("til", t_inv, av[:c])
        kv = unbd(prod("kv", kt, vg, TN, block_diag=False))
        wc_col = unbd(_mmp(_split_bf16(eye * wc, prec["wc"]), [ones_b], TN))

        ur = prod("ur", jnp.concatenate([a_til, rh], axis=0), h_old)
        u = ur[:c] + v_til
        y = ur[c:] + prod("yu", p_rb, u) + av[c:]
        h_new = wc_col * h_old + unbd(prod("hu", bt, u, TN, block_diag=False)) + kv
        return y, h_new

    y_rows = []
    n_chunks = tcb // c
    for b in range(bb):
        h_cur = [h_scr[b, :, gi * gl:(gi + 1) * gl] for gi in range(N_GROUPS)]
        for ci in range(n_chunks):
            rs = slice(b * tcb + ci * c, b * tcb + (ci + 1) * c)
            cum_c = cum[rs]
            cum_last = cum_c[c - 1:c, :]
            e_neg = jnp.exp(-cum_c)
            e_rel = jnp.exp(cum_last - cum_c)
            a_hat = -kkn[rs] * jnp.exp(cum_c - lw[rs])
            r_hat = r[rs] * jnp.exp(cum_c)
            b_hat = b_gate[rs] * e_neg
            k_hat = k2[rs] * e_neg
            b_til = b_gate[rs] * e_rel
            k_til = k2[rs] * e_rel
            w_c = jnp.exp(cum_last)
            v_c = v[rs]
            ys = []
            for gi in range(N_GROUPS):
                sl = slice(gi * gl, (gi + 1) * gl)
                y_g, h_cur[gi] = chunk_group(a_hat[:, sl], r_hat[:, sl], b_hat[:, sl], k_hat[:, sl], b_til[:, sl],
                                             k_til[:, sl], v_c[:, sl], w_c[:, sl], h_cur[gi])
                ys.append(y_g)
            y_rows.append(jnp.concatenate(ys, axis=1))
        for gi in range(N_GROUPS):
            h_scr[b, :, gi * gl:(gi + 1) * gl] = h_cur[gi]

    y = jnp.concatenate(y_rows, axis=0)
    mean = segsum(y) * (1.0 / RWKV_HD)
    yc = y - mean
    var = segsum(yc * yc) * (1.0 / RWKV_HD)
    yn = yc * lax.rsqrt(var + RWKV_LN_EPS) * ln_g + ln_b
    o_ref[...] = ((yn + bonus) * g).reshape(bb, tcb, w)

    @pl.when(j == pl.num_programs(1) - 1)
    def _():
        hfin_ref[...] = h_scr[...]


def _rwkv(p, prev0, h0, mu, vecs, wwa, g2, bdm, bb, tcb):
    bsz, seq, _ = p.shape
    rows = bb * tcb
    t = np.arange(rows)
    ltri = jnp.asarray((t[:, None] >= t[None, :]) & (t[:, None] // CHUNK == t[None, :] // CHUNK), BF16)
    return pl.pallas_call(
        functools.partial(_rwkv_kernel, bb=bb, tcb=tcb),
        grid=(bsz // bb, seq // tcb),
        in_specs=[
            pl.BlockSpec((bb, tcb, RWKV_PROJ), lambda b, j: (b, j, 0)),
            pl.BlockSpec((bb, 1, RWKV_PROJ), lambda b, j: (b, 0, 0)),
            pl.BlockSpec((bb, RWKV_HD, RWKV_WIDTH), lambda b, j: (b, 0, 0)),
            _const_spec((1, RWKV_PROJ)),
            _const_spec((8, RWKV_WIDTH)),
            _const_spec((RWKV_WA_LORA, 2 * RWKV_WIDTH)),
            _const_spec((RWKV_G_LORA, RWKV_WIDTH)),
            _const_spec((rows, rows)),
            _const_spec((GROUP_LANES, GROUP_LANES)),
        ],
        out_specs=[
            pl.BlockSpec((bb, tcb, RWKV_WIDTH), lambda b, j: (b, j, 0)),
            pl.BlockSpec((bb, RWKV_HD, RWKV_WIDTH), lambda b, j: (b, 0, 0)),
        ],
        out_shape=[
            jax.ShapeDtypeStruct((bsz, seq, RWKV_WIDTH), F32),
            jax.ShapeDtypeStruct((bsz, RWKV_HD, RWKV_WIDTH), F32),
        ],
        scratch_shapes=[pltpu.VMEM((bb, RWKV_HD, RWKV_WIDTH), F32), pltpu.VMEM((bb, 1, RWKV_PROJ), F32)],
        compiler_params=_params(("parallel", "arbitrary")),
        name="rwkv7",
    )(p, prev0, h0, mu, vecs, wwa, g2, ltri, bdm)


def _conv_kernel(cu_ref, buf_ref, w_ref, vec_ref, o_ref, bufout_ref, hist, *, tm):
    j = pl.program_id(1)

    @pl.when(j == 0)
    def _():
        hist[0:CONV_PAD, :] = buf_ref[...]

    cu = cu_ref[...]
    hist[CONV_PAD:CONV_PAD + tm, :] = cu[:, :CONV_CH] * _sigmoid(cu[:, CONV_CH:])
    full = hist[...]
    n_hist = tm + CONV_PAD
    off = CONV_PAD - (CONV_WIDTH - 1)
    acc = jnp.broadcast_to(vec_ref[0:1, :], (tm, CONV_CH))
    for res in range(SUBLANES):
        shifted = full if res == 0 else pltpu.roll(full, n_hist - res, 0)
        for i in range(CONV_WIDTH):
            if (off + i) % SUBLANES == res:
                base = off + i - res
                acc = acc + shifted[base:base + tm, :] * w_ref[i:i + 1, :]
    yc = acc - jnp.mean(acc, axis=-1, keepdims=True)
    y = yc * lax.rsqrt(jnp.mean(yc * yc, axis=-1, keepdims=True) + LN_EPS) * vec_ref[1:2, :] + vec_ref[2:3, :]
    o_ref[...] = y * _sigmoid(y)
    tail = full[tm:tm + CONV_PAD, :]
    hist[0:CONV_PAD, :] = tail

    @pl.when(j == pl.num_programs(1) - 1)
    def _():
        bufout_ref[...] = tail[off:, :]


def _conv(cu, buf_pad, conv_w, vecs, tm):
    bsz, seq, _ = cu.shape
    return pl.pallas_call(
        functools.partial(_conv_kernel, tm=tm),
        grid=(bsz, seq // tm),
        in_specs=[
            pl.BlockSpec((None, tm, 2 * CONV_CH), lambda b, j: (b, j, 0)),
            pl.BlockSpec((None, CONV_PAD, CONV_CH), lambda b, j: (b, 0, 0)),
            _const_spec((CONV_PAD, CONV_CH)),
            _const_spec((8, CONV_CH)),
        ],
        out_specs=[
            pl.BlockSpec((None, tm, CONV_CH), lambda b, j: (b, j, 0)),
            pl.BlockSpec((None, CONV_WIDTH - 1, CONV_CH), lambda b, j: (b, 0, 0)),
        ],
        out_shape=[
            jax.ShapeDtypeStruct((bsz, seq, CONV_CH), F32),
            jax.ShapeDtypeStruct((bsz, CONV_WIDTH - 1, CONV_CH), F32),
        ],
        scratch_shapes=[pltpu.VMEM((tm + CONV_PAD, CONV_CH), F32)],
        compiler_params=_params(("parallel", "arbitrary")),
        name="conv_module",
    )(cu, buf_pad, conv_w, vecs)


KV_W = ATT_KV_HEADS * ATT_HD
Q_W = ATT_HEADS * ATT_HD
assert KV_W == LANES


def _rope_small(x, cos_f, sin_a, sin_b):
    n = x.shape[1]
    return x * cos_f + pltpu.roll(x, n - ROT_DIM // 2, 1) * sin_a + pltpu.roll(x, ROT_DIM // 2, 1) * sin_b


def _dup_heads(x):
    lane = lax.broadcasted_iota(jnp.int32, x.shape, 1)
    swapped = pltpu.roll(x, ATT_HD, 1)
    low = lane < ATT_HD
    return jnp.where(low, x, swapped), jnp.where(low, swapped, x)


def _attn_kernel(sink_ref, q_ref, k_ref, v_ref, cos_ref, sa_ref, sb_ref, pk_ref, pv_ref,
                 o_ref, knew_ref, vnew_ref, kd, vd, ktail, vtail, *, tq, has_past):
    j = pl.program_id(1)
    c = CHUNK
    span = WINDOW + c

    def store_rows(r0, n, k_rows, v_rows):
        k0, k1 = _dup_heads(k_rows)
        v0, v1 = _dup_heads(v_rows)
        ones = jnp.ones((n, LANES), BF16)
        kd[0, r0:r0 + n, :] = k0.astype(BF16)
        kd[1, r0:r0 + n, :] = k1.astype(BF16)
        vd[0, r0:r0 + n, :] = jnp.concatenate([v0.astype(BF16), ones], axis=1)
        vd[1, r0:r0 + n, :] = jnp.concatenate([v1.astype(BF16), ones], axis=1)

    @pl.when(j == 0)
    def _():
        store_rows(0, WINDOW, pk_ref[...], pv_ref[...])
        ktail[...] = pk_ref[...]
        vtail[...] = pv_ref[...]

    cos_f, sin_a, sin_b = cos_ref[...], sa_ref[...], sb_ref[...]
    kr = _rope_small(k_ref[...], cos_f, sin_a, sin_b)
    vr = v_ref[...]
    store_rows(WINDOW, tq, kr, vr)
    if tq >= WINDOW:
        ktail[...] = kr[tq - WINDOW:, :]
        vtail[...] = vr[tq - WINDOW:, :]
    else:
        ktail[...] = jnp.concatenate([ktail[tq:, :], kr], axis=0)
        vtail[...] = jnp.concatenate([vtail[tq:, :], vr], axis=0)

    rep = Q_W // KV_W
    qr = _rope_small(q_ref[...], jnp.concatenate([cos_f] * rep, axis=1), jnp.concatenate([sin_a] * rep, axis=1),
                     jnp.concatenate([sin_b] * rep, axis=1)) * (ATT_HD ** -0.5)
    low = lax.broadcasted_iota(jnp.int32, (c, LANES), 1) < ATT_HD
    scol = lax.broadcasted_iota(jnp.int32, (ATT_GROUP * c, span), 1)
    srow_head = lax.shift_right_logical(lax.broadcasted_iota(jnp.int32, (ATT_GROUP * c, 1), 0),
                                        jnp.int32(c.bit_length() - 1))
    for i in range(tq // c):
        for kvh in range(ATT_KV_HEADS):
            stack = []
            for gq in range(ATT_GROUP):
                h = kvh * ATT_GROUP + gq
                pair = qr[i * c:(i + 1) * c, (h // 2) * LANES:(h // 2 + 1) * LANES]
                stack.append(jnp.where(low, pair, 0.0) if h % 2 == 0 else jnp.where(low, 0.0, pair))
            q_stack = jnp.concatenate(stack, axis=0).astype(BF16)
            s = lax.dot_general(q_stack, kd[kvh, i * c:i * c + span, :], NT, preferred_element_type=F32)
            if not has_past:
                blk = j * (tq // c) + i
                s = jnp.where(blk * c + scol < WINDOW, NEG_INF, s)
            sk = jnp.zeros((ATT_GROUP * c, 1), F32)
            for gq in range(ATT_GROUP):
                sk = jnp.where(srow_head == gq, sink_ref[kvh * ATT_GROUP + gq], sk)
            m = jnp.maximum(jnp.max(s, axis=-1, keepdims=True), sk)
            e = jnp.exp(s - m)
            oe = lax.dot_general(e.astype(BF16), vd[kvh, i * c:i * c + span, :], NN, preferred_element_type=F32)
            o = oe[:, :LANES] / (oe[:, LANES:] + jnp.exp(sk - m))
            for t in range(ATT_GROUP // 2):
                h = kvh * ATT_GROUP + 2 * t
                pair_out = jnp.where(low, o[2 * t * c:(2 * t + 1) * c], o[(2 * t + 1) * c:(2 * t + 2) * c])
                o_ref[i * c:(i + 1) * c, (h // 2) * LANES:(h // 2 + 1) * LANES] = pair_out
    kd[:, 0:WINDOW, :] = kd[:, tq:tq + WINDOW, :]
    vd[:, 0:WINDOW, :] = vd[:, tq:tq + WINDOW, :]

    @pl.when(j == pl.num_programs(1) - 1)
    def _():
        knew_ref[...] = ktail[...]
        vnew_ref[...] = vtail[...]


def _attention(qkv, cos_f, sin_a, sin_b, past_k, past_v, sink, tq, has_past):
    bsz, seq, _ = qkv.shape
    nq = Q_W // KV_W
    tab = pl.BlockSpec((tq, KV_W), lambda b, j: (j, 0))
    past = pl.BlockSpec((None, WINDOW, KV_W), lambda b, j: (b, 0, 0))
    return pl.pallas_call(
        functools.partial(_attn_kernel, tq=tq, has_past=has_past),
        grid=(bsz, seq // tq),
        in_specs=[
            pl.BlockSpec(memory_space=pltpu.SMEM),
            pl.BlockSpec((None, tq, Q_W), lambda b, j: (b, j, 0)),
            pl.BlockSpec((None, tq, KV_W), lambda b, j: (b, j, nq)),
            pl.BlockSpec((None, tq, KV_W), lambda b, j: (b, j, nq + 1)),
            tab, tab, tab, past, past,
        ],
        out_specs=[pl.BlockSpec((None, tq, Q_W), lambda b, j: (b, j, 0)), past, past],
        out_shape=[
            jax.ShapeDtypeStruct((bsz, seq, Q_W), F32),
            jax.ShapeDtypeStruct((bsz, WINDOW, KV_W), F32),
            jax.ShapeDtypeStruct((bsz, WINDOW, KV_W), F32),
        ],
        scratch_shapes=[
            pltpu.VMEM((ATT_KV_HEADS, WINDOW + tq, LANES), BF16),
            pltpu.VMEM((ATT_KV_HEADS, WINDOW + tq, 2 * LANES), BF16),
            pltpu.VMEM((WINDOW, KV_W), F32),
            pltpu.VMEM((WINDOW, KV_W), F32),
        ],
        compiler_params=_params(("parallel", "arbitrary")),
        name="swa_attention",
    )(sink, qkv, qkv, qkv, cos_f, sin_a, sin_b, past_k, past_v)


def _rope_tables(pos0, seq, rot_dim, theta):
    half = rot_dim // 2
    inv = np.exp(np.arange(half, dtype=np.float64) * (-2.0 * math.log(theta) / rot_dim))
    ang = np.arange(pos0, pos0 + seq).astype(np.float64)[:, None] * inv[None, :]
    return jnp.asarray(np.cos(ang), F32), jnp.asarray(np.sin(ang), F32)


def _largest_tile(n, cap):
    t = cap
    while n % t:
        t //= 2
    return t


def _prepare(prm):
    w = RWKV_WIDTH
    qkvg_w = 2 * RET_HEADS * RET_DK + 2 * RET_WIDTH
    cu_w = 2 * CONV_CH
    pp = {}
    pp["w_in_ab_a"] = prm["w_in_ab"][0][:, :qkvg_w].astype(BF16)
    pp["w_in_ab_b"] = prm["w_in_ab"][0][:, qkvg_w:].astype(BF16)
    pp["w_out_ab_a"] = prm["w_out_ab"][0][:RET_WIDTH].astype(BF16)
    pp["w_out_ab_b"] = prm["w_out_ab"][0][RET_WIDTH:].astype(BF16)
    pp["w_in_cd_a"] = prm["w_in_cd"][0][:, :cu_w].astype(BF16)
    pp["w_in_cd_b"] = prm["w_in_cd"][0][:, cu_w:].astype(BF16)
    pp["w_out_cd_a"] = prm["w_out_cd"][0][:CONV_CH].astype(BF16)
    pp["w_out_cd_b"] = prm["w_out_cd"][0][CONV_CH:].astype(BF16)
    for name in ("ffn_gate", "ffn_up", "ffn_down"):
        pp[name] = prm[name].astype(BF16)
    zeros = jnp.zeros((RWKV_W_LORA, w), F32)
    pp["rwkv_wwa"] = jnp.concatenate(
        [jnp.concatenate([prm["rwkv_w2"][0], zeros], axis=1), jnp.concatenate([zeros, prm["rwkv_a2"][0]], axis=1)],
        axis=0).astype(BF16)
    pp["rwkv_g2"] = prm["rwkv_g2"][0].astype(BF16)
    pp["rwkv_vecs"] = jnp.stack(
        [prm["rwkv_w0"][0], prm["rwkv_a0"][0], prm["rwkv_kk"][0], prm["rwkv_ka"][0], prm["rwkv_rk"][0].reshape(w),
         prm["rwkv_ln_g"][0], prm["rwkv_ln_b"][0], jnp.zeros((w,), F32)], axis=0)
    pp["rwkv_mu"] = prm["rwkv_mu"][0].reshape(1, RWKV_PROJ)
    head = np.arange(GROUP_LANES) // RWKV_HD
    pp["bdm"] = jnp.asarray(head[:, None] == head[None, :], BF16)
    pp["conv_w"] = jnp.concatenate([prm["conv_w"][0], jnp.zeros((CONV_PAD - CONV_WIDTH, CONV_CH), F32)], axis=0)
    pp["conv_vecs"] = jnp.concatenate(
        [jnp.stack([prm["conv_b"][0], prm["conv_ln_g"][0], prm["conv_ln_b"][0]], axis=0),
         jnp.zeros((5, CONV_CH), F32)], axis=0)
    return pp


def _rwkv_tiling(bsz, seq):
    tcb = _largest_tile(seq, 4 * CHUNK)
    bb = _largest_tile(bsz, max(1, (4 * CHUNK) // tcb))
    return bb, tcb


def _trunk(x, pos0, has_past, st_ret, st_wkv, st_shift, st_conv, st_k, st_v, prm, pp):
    bsz, seq, _ = x.shape
    tokens = bsz * seq
    tm = _largest_tile(tokens, 256)
    row = lambda v: v.reshape(1, -1)

    x2 = x.reshape(tokens, D_MODEL)
    qkvg, p = _in_proj(x2, row(prm["norm_mix"][0]), pp["w_in_ab_a"], pp["w_in_ab_b"], tm)
    qkvg = qkvg.reshape(bsz, seq, -1)
    p = p.reshape(bsz, seq, RWKV_PROJ)
    cos, sin = _rope_tables(pos0, seq, RET_DK, RET_THETA)
    cos_t = jnp.concatenate([cos, cos], axis=1)
    sin_t = jnp.concatenate([-sin, sin], axis=1)
    o_ret, ret_new = _retention(qkvg, cos_t, sin_t, row(prm["ret_gn_g"][0]), st_ret, _largest_tile(seq, 256))
    h0 = st_wkv.transpose(0, 3, 1, 2).reshape(bsz, RWKV_HD, RWKV_WIDTH)
    bb, tcb = _rwkv_tiling(bsz, seq)
    o_rwkv, h_new = _rwkv(p, st_shift.reshape(bsz, 1, RWKV_PROJ), h0, pp["rwkv_mu"], pp["rwkv_vecs"],
                          pp["rwkv_wwa"], pp["rwkv_g2"], pp["bdm"], bb, tcb)
    wkv_new = h_new.reshape(bsz, RWKV_HD, RWKV_HEADS, RWKV_HD).transpose(0, 2, 3, 1)
    shift_new = p[:, -1]
    x2 = _post(x2, o_ret.reshape(tokens, -1), o_rwkv.reshape(tokens, -1), pp["w_out_ab_a"], pp["w_out_ab_b"],
               row(prm["norm_ffn"][0]), pp["ffn_gate"][0], pp["ffn_up"][0], pp["ffn_down"][0],
               row(prm["norm_final"]), tm, final=False)

    cu, qkv = _in_proj(x2, row(prm["norm_mix"][1]), pp["w_in_cd_a"], pp["w_in_cd_b"], tm)
    cu = cu.reshape(bsz, seq, -1)
    qkv = qkv.reshape(bsz, seq, -1)
    buf_pad = jnp.concatenate([jnp.zeros((bsz, CONV_PAD - (CONV_WIDTH - 1), CONV_CH), F32), st_conv], axis=1)
    o_conv, conv_new = _conv(cu, buf_pad, pp["conv_w"], pp["conv_vecs"], _largest_tile(seq, 256))
    cos, sin = _rope_tables(pos0, seq, ROT_DIM, ROPE_THETA)
    half = ROT_DIM // 2
    pad = jnp.zeros((seq, ATT_HD - ROT_DIM), F32)
    z8 = jnp.zeros((seq, half), F32)
    cos_f = jnp.concatenate([cos, cos, pad + 1.0], axis=1)
    sin_a = jnp.concatenate([-sin, z8, pad], axis=1)
    sin_b = jnp.concatenate([z8, sin, pad], axis=1)
    tile2 = lambda t: jnp.concatenate([t] * ATT_KV_HEADS, axis=1)
    o_att, k_new, v_new = _attention(qkv, tile2(cos_f), tile2(sin_a), tile2(sin_b),
                                     st_k.reshape(bsz, WINDOW, KV_W), st_v.reshape(bsz, WINDOW, KV_W),
                                     prm["attn_sink"][0], _largest_tile(seq, 256), has_past)
    y2 = _post(x2, o_conv.reshape(tokens, -1), o_att.reshape(tokens, -1), pp["w_out_cd_a"], pp["w_out_cd_b"],
               row(prm["norm_ffn"][1]), pp["ffn_gate"][1], pp["ffn_up"][1], pp["ffn_down"][1],
               row(prm["norm_final"]), tm, final=True)
    kv_shape = (bsz, WINDOW, ATT_KV_HEADS, ATT_HD)
    return (y2.reshape(bsz, seq, D_MODEL), ret_new[None], wkv_new[None], shift_new[None], conv_new[None],
            k_new.reshape(kv_shape)[None], v_new.reshape(kv_shape)[None])


def kernel(x_prompt, x_sample, state_ret, state_wkv, state_shift, state_conv, cache_k, cache_v, norm_mix, norm_ffn, norm_final, w_in_ab, w_out_ab, ret_gn_g, rwkv_mu, rwkv_w0, rwkv_w2, rwkv_a0, rwkv_a2, rwkv_g2, rwkv_kk, rwkv_ka, rwkv_rk, rwkv_ln_g, rwkv_ln_b, w_in_cd, w_out_cd, conv_w, conv_b, conv_ln_g, conv_ln_b, attn_sink, ffn_gate, ffn_up, ffn_down):
    prm = {
        'norm_mix': norm_mix, 'norm_ffn': norm_ffn, 'norm_final': norm_final,
        'w_in_ab': w_in_ab, 'w_out_ab': w_out_ab, 'ret_gn_g': ret_gn_g,
        'rwkv_mu': rwkv_mu, 'rwkv_w0': rwkv_w0, 'rwkv_w2': rwkv_w2, 'rwkv_a0': rwkv_a0, 'rwkv_a2': rwkv_a2,
        'rwkv_g2': rwkv_g2, 'rwkv_kk': rwkv_kk, 'rwkv_ka': rwkv_ka, 'rwkv_rk': rwkv_rk,
        'rwkv_ln_g': rwkv_ln_g, 'rwkv_ln_b': rwkv_ln_b,
        'w_in_cd': w_in_cd, 'w_out_cd': w_out_cd, 'conv_w': conv_w, 'conv_b': conv_b,
        'conv_ln_g': conv_ln_g, 'conv_ln_b': conv_ln_b, 'attn_sink': attn_sink,
        'ffn_gate': ffn_gate, 'ffn_up': ffn_up, 'ffn_down': ffn_down,
    }
    pp = _prepare(prm)
    bsz = x_prompt.shape[0]
    dt = x_prompt.dtype
    z_ret = jnp.zeros((bsz, RET_HEADS, RET_DK, RET_DV), dt)
    z_wkv = jnp.zeros((bsz, RWKV_HEADS, RWKV_HD, RWKV_HD), dt)
    z_shift = jnp.zeros((bsz, RWKV_PROJ), dt)
    z_conv = jnp.zeros((bsz, CONV_WIDTH - 1, CONV_CH), dt)
    z_kv = jnp.zeros((bsz, WINDOW, ATT_KV_HEADS, ATT_HD), dt)
    out_p = _trunk(x_prompt, 0, False, z_ret, z_wkv, z_shift, z_conv, z_kv, z_kv, prm, pp)
    out_s = _trunk(x_sample, PAST_LEN, True, state_ret[0], state_wkv[0], state_shift[0], state_conv[0],
                   cache_k[0], cache_v[0], prm, pp)
    return (out_p[0], out_s[0]) + tuple(out_p[1:]) + tuple(out_s[1:])
```

```python
import functools
import math

import jax
import jax.numpy as jnp
import numpy as np
from jax import lax
from jax.experimental import pallas as pl
from jax.experimental.pallas import tpu as pltpu

F32 = jnp.float32
BF16 = jnp.bfloat16

D_MODEL = 1024
CHUNK = 64
RMS_EPS = 1e-6
LN_EPS = 1e-5

RET_HEADS = 4
RET_DK = 128
RET_DV = 128
RET_WIDTH = RET_HEADS * RET_DV
RET_THETA = 10000.0

RWKV_HEADS = 8
RWKV_HD = 64
RWKV_WIDTH = RWKV_HEADS * RWKV_HD
RWKV_W_LORA = 64
RWKV_A_LORA = 64
RWKV_WA_LORA = RWKV_W_LORA + RWKV_A_LORA
RWKV_G_LORA = 128
RWKV_PROJ = 3 * RWKV_WIDTH + RWKV_WA_LORA + RWKV_G_LORA
RWKV_LN_EPS = 64e-5

CONV_CH = 512
CONV_WIDTH = 31
CONV_PAD = 32
SUBLANES = 8

ATT_HEADS = 8
ATT_KV_HEADS = 2
ATT_HD = 64
ATT_GROUP = ATT_HEADS // ATT_KV_HEADS
WINDOW = 128
ROT_DIM = ATT_HD // 4
ROPE_THETA = 500000.0
NEG_INF = -1e30
LANES = 128

PAST_LEN = 4096

V7X_VMEM_LIMIT_BYTES = 56 * 1024 * 1024

NN = (((1,), (0,)), ((), ()))
NT = (((1,), (1,)), ((), ()))
TN = (((0,), (0,)), ((), ()))


def _split_bf16(x, n):
    pieces = []
    rem = x
    for i in range(n):
        piece = rem.astype(BF16)
        pieces.append(piece)
        if i + 1 < n:
            rem = rem - piece.astype(F32)
    return pieces


def _mmp(pa, pb, dn=NN):
    n = max(len(pa), len(pb))
    acc = None
    for i, x in enumerate(pa):
        for j, y in enumerate(pb):
            if i + j < n:
                t = lax.dot_general(x, y, dn, preferred_element_type=F32)
                acc = t if acc is None else acc + t
    return acc


def _mm(a, b, dn=NN, na=1, nb=1):
    return _mmp(_split_bf16(a, na), _split_bf16(b, nb), dn)


def _rms(x, g):
    ms = jnp.mean(x * x, axis=-1, keepdims=True)
    return (x * lax.rsqrt(ms + RMS_EPS)) * g


def _sigmoid(x):
    return 1.0 / (1.0 + jnp.exp(-x))


def _params(sem):
    return pltpu.CompilerParams(dimension_semantics=sem, vmem_limit_bytes=V7X_VMEM_LIMIT_BYTES)


def _const_spec(shape):
    nd = len(shape)
    return pl.BlockSpec(shape, lambda *_: (0,) * nd, pipeline_mode=pl.Buffered(1))


def _in_proj_kernel(x_ref, g_ref, wa_ref, wb_ref, oa_ref, ob_ref):
    h = _rms(x_ref[...], g_ref[...]).astype(BF16)
    oa_ref[...] = jnp.dot(h, wa_ref[...], preferred_element_type=F32)
    ob_ref[...] = jnp.dot(h, wb_ref[...], preferred_element_type=F32)


def _in_proj(x2, g, wa, wb, tm):
    t = x2.shape[0]
    na, nb = wa.shape[1], wb.shape[1]
    return pl.pallas_call(
        _in_proj_kernel,
        grid=(t // tm,),
        in_specs=[
            pl.BlockSpec((tm, D_MODEL), lambda i: (i, 0)),
            _const_spec((1, D_MODEL)),
            _const_spec((D_MODEL, na)),
            _const_spec((D_MODEL, nb)),
        ],
        out_specs=[
            pl.BlockSpec((tm, na), lambda i: (i, 0)),
            pl.BlockSpec((tm, nb), lambda i: (i, 0)),
        ],
        out_shape=[jax.ShapeDtypeStruct((t, na), F32), jax.ShapeDtypeStruct((t, nb), F32)],
        compiler_params=_params(("parallel",)),
        name="in_proj",
    )(x2, g, wa, wb)


def _post_kernel(x_ref, a_ref, b_ref, wa_ref, wb_ref, g_ref, wg_ref, wu_ref, wd_ref, gf_ref, o_ref, *, final):
    x = x_ref[...]
    x = x + jnp.dot(a_ref[...].astype(BF16), wa_ref[...], preferred_element_type=F32)
    x = x + jnp.dot(b_ref[...].astype(BF16), wb_ref[...], preferred_element_type=F32)
    h = _rms(x, g_ref[...]).astype(BF16)
    gate = jnp.dot(h, wg_ref[...], preferred_element_type=F32)
    up = jnp.dot(h, wu_ref[...], preferred_element_type=F32)
    act = (gate * _sigmoid(gate)) * up
    x = x + jnp.dot(act.astype(BF16), wd_ref[...], preferred_element_type=F32)
    if final:
        x = _rms(x, gf_ref[...])
    o_ref[...] = x


def _post(x2, a2, b2, wa, wb, g, wg, wu, wd, gf, tm, final):
    t = x2.shape[0]
    ka, kb, dff = a2.shape[1], b2.shape[1], wg.shape[1]
    return pl.pallas_call(
        functools.partial(_post_kernel, final=final),
        grid=(t // tm,),
        in_specs=[
            pl.BlockSpec((tm, D_MODEL), lambda i: (i, 0)),
            pl.BlockSpec((tm, ka), lambda i: (i, 0)),
            pl.BlockSpec((tm, kb), lambda i: (i, 0)),
            _const_spec((ka, D_MODEL)),
            _const_spec((kb, D_MODEL)),
            _const_spec((1, D_MODEL)),
            _const_spec((D_MODEL, dff)),
            _const_spec((D_MODEL, dff)),
            _const_spec((dff, D_MODEL)),
            _const_spec((1, D_MODEL)),
        ],
        out_specs=pl.BlockSpec((tm, D_MODEL), lambda i: (i, 0)),
        out_shape=jax.ShapeDtypeStruct((t, D_MODEL), F32),
        compiler_params=_params(("parallel",)),
        name="post_final" if final else "post",
    )(x2, a2, b2, wa, wb, g, wg, wu, wd, gf)


def _ret_kernel(cdec_ref, q_ref, k_ref, v_ref, gt_ref, cos_ref, sin_ref, dmat_ref, qdec_ref, kdec_ref, gn_ref, r0_ref,
                o_ref, rfin_ref, r_scr):
    j = pl.program_id(1)

    @pl.when(j == 0)
    def _():
        r_scr[...] = r0_ref[...]

    cos = cos_ref[...]
    sin = sin_ref[...]
    for h in range(RET_HEADS):
        sl = slice(h * RET_DK, (h + 1) * RET_DK)
        q = q_ref[:, sl]
        k = k_ref[:, sl]
        v = v_ref[:, sl]
        qr = q * cos + pltpu.roll(q, RET_DK // 2, 1) * sin
        kr = (k * cos + pltpu.roll(k, RET_DK // 2, 1) * sin) * (RET_DK ** -0.5)
        inner = _mm(qr, kr, NT) * dmat_ref[h]
        r = r_scr[h]
        o = _mm(inner, v) + _mm(qr, r) * qdec_ref[h]
        r_scr[h] = r * cdec_ref[h] + _mm(kr * kdec_ref[h], v, TN)
        oc = o - jnp.mean(o, axis=-1, keepdims=True)
        y = oc * lax.rsqrt(jnp.mean(oc * oc, axis=-1, keepdims=True) + LN_EPS) * gn_ref[:, sl]
        gt = gt_ref[:, sl]
        o_ref[:, sl] = (gt * _sigmoid(gt)) * y

    @pl.when(j == pl.num_programs(1) - 1)
    def _():
        rfin_ref[...] = r_scr[...]


def _ret_tables(c):
    lg = np.log1p(-np.exp2(-5.0 - np.arange(RET_HEADS, dtype=np.float64)))
    idx = np.arange(c, dtype=np.float64)
    diff = idx[:, None] - idx[None, :]
    dmat = np.where(diff >= 0, np.exp(lg[:, None, None] * np.maximum(diff, 0.0)), 0.0)
    qdec = np.exp(lg[:, None] * (idx + 1.0))
    kdec = np.exp(lg[:, None] * (c - 1.0 - idx))
    qdec = np.broadcast_to(qdec[:, :, None], (RET_HEADS, c, RET_DV))
    kdec = np.broadcast_to(kdec[:, :, None], (RET_HEADS, c, RET_DK))
    cdec = np.exp(lg * c)
    return tuple(jnp.asarray(t, F32) for t in (dmat, qdec, kdec, cdec))


def _retention(qkvg, cos, sin, gn, r0, c):
    bsz, seq, _ = qkvg.shape
    dmat, qdec, kdec, cdec = _ret_tables(c)
    col = lambda n: pl.BlockSpec((None, c, RET_WIDTH), lambda b, j, n=n: (b, j, n))
    return pl.pallas_call(
        _ret_kernel,
        grid=(bsz, seq // c),
        in_specs=[
            pl.BlockSpec(memory_space=pltpu.SMEM),
            col(0), col(1), col(2), col(3),
            pl.BlockSpec((c, RET_DK), lambda b, j: (j, 0)),
            pl.BlockSpec((c, RET_DK), lambda b, j: (j, 0)),
            _const_spec((RET_HEADS, c, c)),
            _const_spec((RET_HEADS, c, RET_DV)),
            _const_spec((RET_HEADS, c, RET_DK)),
            _const_spec((1, RET_WIDTH)),
            pl.BlockSpec((None, RET_HEADS, RET_DK, RET_DV), lambda b, j: (b, 0, 0, 0)),
        ],
        out_specs=[
            pl.BlockSpec((None, c, RET_WIDTH), lambda b, j: (b, j, 0)),
            pl.BlockSpec((None, RET_HEADS, RET_DK, RET_DV), lambda b, j: (b, 0, 0, 0)),
        ],
        out_shape=[
            jax.ShapeDtypeStruct((bsz, seq, RET_WIDTH), F32),
            jax.ShapeDtypeStruct((bsz, RET_HEADS, RET_DK, RET_DV), F32),
        ],
        scratch_shapes=[pltpu.VMEM((RET_HEADS, RET_DK, RET_DV), F32)],
        compiler_params=_params(("parallel", "arbitrary")),
        name="retention",
    )(cdec, qkvg, qkvg, qkvg, qkvg, cos, sin, dmat, qdec, kdec, gn, r0)


GROUP_LANES = 256
GROUP_HEADS = GROUP_LANES // RWKV_HD
N_GROUPS = RWKV_WIDTH // GROUP_LANES
RWKV_CHUNKS_PER_STEP = 4

RWKV_PREC = {
    "seg": 2,
    "cum": 2,
    "p12": (1, 1),
    "dbl": (1, 1),
    "av": (1, 1),
    "til": (1, 1),
    "wc": 2,
    "ur": (1, 1),
    "yu": (1, 1),
    "hu": (1, 1),
}


def _rwkv_kernel(p_ref, prev0_ref, h0_ref, mu_ref, vec_ref, wwa_ref, g2_ref, ltri_ref, bdm_ref,
                 o_ref, hfin_ref, h_scr, prev_scr, *, bb, tcb):
    j = pl.program_id(1)
    c = CHUNK
    w = RWKV_WIDTH
    gl = GROUP_LANES
    rows = bb * tcb
    prec = RWKV_PREC

    @pl.when(j == 0)
    def _():
        h_scr[...] = h0_ref[...]
        prev_scr[...] = prev0_ref[...]

    p = p_ref[...].reshape(rows, RWKV_PROJ)
    row = lax.broadcasted_iota(jnp.int32, p.shape, 0)
    p_prev = pltpu.roll(p, 1, 0)
    for b in range(bb):
        p_prev = jnp.where(row == b * tcb, prev_scr[b], p_prev)
        prev_scr[b] = p[(b + 1) * tcb - 1:(b + 1) * tcb, :]
    xs = p + (p_prev - p) * mu_ref[...]

    r = xs[:, 0:w]
    k = xs[:, w:2 * w]
    v = xs[:, 2 * w:3 * w]
    wa_in = xs[:, 3 * w:3 * w + RWKV_WA_LORA]
    lane = lax.broadcasted_iota(jnp.int32, wa_in.shape, 1)
    wa_in = jnp.where(lane < RWKV_W_LORA, jnp.tanh(wa_in), wa_in)
    wa = _mm(wa_in, wwa_ref[...])
    g = _mm(_sigmoid(xs[:, 3 * w + RWKV_WA_LORA:]), g2_ref[...])

    w0 = vec_ref[0:1, :]
    a0 = vec_ref[1:2, :]
    k_k = vec_ref[2:3, :]
    k_a = vec_ref[3:4, :]
    r_k = vec_ref[4:5, :]
    ln_g = vec_ref[5:6, :]
    ln_b = vec_ref[6:7, :]
    bdm = bdm_ref[...]

    def segsum(t):
        return jnp.concatenate(
            [_mmp(_split_bf16(t[:, gi * gl:(gi + 1) * gl], prec["seg"]), [bdm]) for gi in range(N_GROUPS)], axis=1)

    z = -(w0 + wa[:, :w])
    softplus = jnp.maximum(z, 0.0) + jnp.log1p(jnp.exp(-jnp.abs(z)))
    lw = -jnp.exp(-softplus - 0.5)
    a = _sigmoid(a0 + wa[:, w:])
    kk = k * k_k
    kkn = kk / jnp.maximum(jnp.sqrt(segsum(kk * kk)), 1e-12)
    k2 = k * (1.0 + (a - 1.0) * k_a)
    bonus = segsum(r * k2 * r_k) * v
    b_gate = kkn * a
    cum = _mmp([ltri_ref[...]], _split_bf16(lw, prec["cum"]))

    shift = jnp.int32(RWKV_HD.bit_length() - 1)
    lane_blk = lax.shift_right_logical(lax.broadcasted_iota(jnp.int32, (c, gl), 1), shift)
    trow = lax.broadcasted_iota(jnp.int32, (c, gl), 0)
    tcol = jnp.bitwise_and(lax.broadcasted_iota(jnp.int32, (c, gl), 1), jnp.int32(RWKV_HD - 1))
    strict = trow > tcol
    incl = trow >= tcol
    eye = jnp.where(trow == tcol, 1.0, 0.0).astype(F32)
    eye_b = eye.astype(BF16)

    def bd(pieces):
        return [jnp.concatenate([q] * GROUP_HEADS, axis=0) * bdm for q in pieces]

    def unbd(full):
        out = full[0:RWKV_HD]
        for hh in range(1, GROUP_HEADS):
            out = jnp.where(lane_blk == hh, full[hh * RWKV_HD:(hh + 1) * RWKV_HD], out)
        return out

    def prod(key, x, y, dn=NN, block_diag=True):
        na, nb = prec[key]
        pb = _split_bf16(y, nb)
        return _mmp(_split_bf16(x, na), bd(pb) if block_diag else pb, dn)

    n_chunks = tcb // c
    units = []
    for ci in range(rows // c):
        rs = slice(ci * c, (ci + 1) * c)
        cum_c = cum[rs]
        cum_last = cum_c[c - 1:c, :]
        e_neg = jnp.exp(-cum_c)
        e_rel = jnp.exp(cum_last - cum_c)
        a_hat = -kkn[rs] * jnp.exp(cum_c - lw[rs])
        r_hat = r[rs] * jnp.exp(cum_c)
        b_hat = b_gate[rs] * e_neg
        k_hat = k2[rs] * e_neg
        b_til = b_gate[rs] * e_rel
        k_til = k2[rs] * e_rel
        w_c = jnp.exp(cum_last)
        v_c = v[rs]
        for gi in range(N_GROUPS):
            sl = slice(gi * gl, (gi + 1) * gl)
            units.append(dict(ah=a_hat[:, sl], rh=r_hat[:, sl], bh=b_hat[:, sl], kh=k_hat[:, sl], bt=b_til[:, sl],
                              kt=k_til[:, sl], vg=v_c[:, sl], wc=w_c[:, sl]))
    for un in units:
        un["lhs"] = jnp.concatenate([un["ah"], un["rh"]], axis=0)
    for un in units:
        p1 = prod("p12", un["lhs"], un["bh"], NT)
        un["a_ab"] = jnp.where(strict, p1[:c], 0.0)
        un["p_rb"] = jnp.where(incl, p1[c:], 0.0)
    for un in units:
        p2 = prod("p12", un["lhs"], un["kh"], NT)
        un["a_ak"] = jnp.where(strict, p2[:c], 0.0)
        un["p_rk"] = jnp.where(incl, p2[c:], 0.0)
    for un in units:
        un["t_inv"] = eye + un["a_ab"]
        un["pw"] = prod("dbl", un["a_ab"], un["a_ab"])
    for _ in range(int(math.log2(c)) - 2):
        for un in units:
            tp = prod("dbl", jnp.concatenate([un["t_inv"], un["pw"]], axis=0), un["pw"])
            un["t_inv"] = un["t_inv"] + tp[:c]
            un["pw"] = tp[c:]
    for un in units:
        un["t_inv"] = un["t_inv"] + prod("dbl", un["t_inv"], un["pw"])
    for un in units:
        un["av"] = prod("av", jnp.concatenate([un["a_ak"], un["p_rk"]], axis=0), un["vg"])
    for un in units:
        z = [jnp.broadcast_to(q, (gl, gl)) * bdm for q in _split_bf16(un["wc"], prec["wc"])]
        un["wc_col"] = _mmp([eye_b], z, NT)
        un["btkt"] = jnp.concatenate([un["bt"], un["kt"]], axis=0)
    for un in units:
        un["a_til"] = prod("til", un["t_inv"], un["ah"])
    for un in units:
        un["v_til"] = prod("til", un["t_inv"], un["av"][:c])

    y_rows = []
    for b in range(bb):
        h_cur = [h_scr[b, :, gi * gl:(gi + 1) * gl] for gi in range(N_GROUPS)]
        for ci in range(n_chunks):
            ys = []
            for gi in range(N_GROUPS):
                un = units[(b * n_chunks + ci) * N_GROUPS + gi]
                h_old = h_cur[gi]
                ur = prod("ur", jnp.concatenate([un["a_til"], un["rh"]], axis=0), h_old)
                u = ur[:c] + un["v_til"]
                ys.append(ur[c:] + prod("yu", un["p_rb"], u) + un["av"][c:])
                upd = prod("hu", un["btkt"], jnp.concatenate([u, un["vg"]], axis=0), TN, block_diag=False)
                h_cur[gi] = un["wc_col"] * h_old + unbd(upd)
            y_rows.append(jnp.concatenate(ys, axis=1))
        for gi in range(N_GROUPS):
            h_scr[b, :, gi * gl:(gi + 1) * gl] = h_cur[gi]

    y = jnp.concatenate(y_rows, axis=0)
    mean = segsum(y) * (1.0 / RWKV_HD)
    yc = y - mean
    var = segsum(yc * yc) * (1.0 / RWKV_HD)
    yn = yc * lax.rsqrt(var + RWKV_LN_EPS) * ln_g + ln_b
    o_ref[...] = ((yn + bonus) * g).reshape(bb, tcb, w)

    @pl.when(j == pl.num_programs(1) - 1)
    def _():
        hfin_ref[...] = h_scr[...]


def _rwkv(p, prev0, h0, mu, vecs, wwa, g2, bdm, bb, tcb):
    bsz, seq, _ = p.shape
    rows = bb * tcb
    t = np.arange(rows)
    ltri = jnp.asarray((t[:, None] >= t[None, :]) & (t[:, None] // CHUNK == t[None, :] // CHUNK), BF16)
    return pl.pallas_call(
        functools.partial(_rwkv_kernel, bb=bb, tcb=tcb),
        grid=(bsz // bb, seq // tcb),
        in_specs=[
            pl.BlockSpec((bb, tcb, RWKV_PROJ), lambda b, j: (b, j, 0)),
            pl.BlockSpec((bb, 1, RWKV_PROJ), lambda b, j: (b, 0, 0)),
            pl.BlockSpec((bb, RWKV_HD, RWKV_WIDTH), lambda b, j: (b, 0, 0)),
            _const_spec((1, RWKV_PROJ)),
            _const_spec((8, RWKV_WIDTH)),
            _const_spec((RWKV_WA_LORA, 2 * RWKV_WIDTH)),
            _const_spec((RWKV_G_LORA, RWKV_WIDTH)),
            _const_spec((rows, rows)),
            _const_spec((GROUP_LANES, GROUP_LANES)),
        ],
        out_specs=[
            pl.BlockSpec((bb, tcb, RWKV_WIDTH), lambda b, j: (b, j, 0)),
            pl.BlockSpec((bb, RWKV_HD, RWKV_WIDTH), lambda b, j: (b, 0, 0)),
        ],
        out_shape=[
            jax.ShapeDtypeStruct((bsz, seq, RWKV_WIDTH), F32),
            jax.ShapeDtypeStruct((bsz, RWKV_HD, RWKV_WIDTH), F32),
        ],
        scratch_shapes=[pltpu.VMEM((bb, RWKV_HD, RWKV_WIDTH), F32), pltpu.VMEM((bb, 1, RWKV_PROJ), F32)],
        compiler_params=_params(("parallel", "arbitrary")),
        name="rwkv7",
    )(p, prev0, h0, mu, vecs, wwa, g2, ltri, bdm)


def _conv_kernel(cu_ref, buf_ref, w_ref, vec_ref, o_ref, bufout_ref, hist, *, tm):
    j = pl.program_id(1)

    @pl.when(j == 0)
    def _():
        hist[0:CONV_PAD, :] = buf_ref[...]

    cu = cu_ref[...]
    hist[CONV_PAD:CONV_PAD + tm, :] = cu[:, :CONV_CH] * _sigmoid(cu[:, CONV_CH:])
    full = hist[...]
    n_hist = tm + CONV_PAD
    off = CONV_PAD - (CONV_WIDTH - 1)
    acc = jnp.broadcast_to(vec_ref[0:1, :], (tm, CONV_CH))
    for res in range(SUBLANES):
        shifted = full if res == 0 else pltpu.roll(full, n_hist - res, 0)
        for i in range(CONV_WIDTH):
            if (off + i) % SUBLANES == res:
                base = off + i - res
                acc = acc + shifted[base:base + tm, :] * w_ref[i:i + 1, :]
    yc = acc - jnp.mean(acc, axis=-1, keepdims=True)
    y = yc * lax.rsqrt(jnp.mean(yc * yc, axis=-1, keepdims=True) + LN_EPS) * vec_ref[1:2, :] + vec_ref[2:3, :]
    o_ref[...] = y * _sigmoid(y)
    tail = full[tm:tm + CONV_PAD, :]
    hist[0:CONV_PAD, :] = tail

    @pl.when(j == pl.num_programs(1) - 1)
    def _():
        bufout_ref[...] = tail[off:, :]


def _conv(cu, buf_pad, conv_w, vecs, tm):
    bsz, seq, _ = cu.shape
    return pl.pallas_call(
        functools.partial(_conv_kernel, tm=tm),
        grid=(bsz, seq // tm),
        in_specs=[
            pl.BlockSpec((None, tm, 2 * CONV_CH), lambda b, j: (b, j, 0)),
            pl.BlockSpec((None, CONV_PAD, CONV_CH), lambda b, j: (b, 0, 0)),
            _const_spec((CONV_PAD, CONV_CH)),
            _const_spec((8, CONV_CH)),
        ],
        out_specs=[
            pl.BlockSpec((None, tm, CONV_CH), lambda b, j: (b, j, 0)),
            pl.BlockSpec((None, CONV_WIDTH - 1, CONV_CH), lambda b, j: (b, 0, 0)),
        ],
        out_shape=[
            jax.ShapeDtypeStruct((bsz, seq, CONV_CH), F32),
            jax.ShapeDtypeStruct((bsz, CONV_WIDTH - 1, CONV_CH), F32),
        ],
        scratch_shapes=[pltpu.VMEM((tm + CONV_PAD, CONV_CH), F32)],
        compiler_params=_params(("parallel", "arbitrary")),
        name="conv_module",
    )(cu, buf_pad, conv_w, vecs)


KV_W = ATT_KV_HEADS * ATT_HD
Q_W = ATT_HEADS * ATT_HD
assert KV_W == LANES


def _rope_small(x, cos_f, sin_a, sin_b):
    n = x.shape[1]
    return x * cos_f + pltpu.roll(x, n - ROT_DIM // 2, 1) * sin_a + pltpu.roll(x, ROT_DIM // 2, 1) * sin_b


def _dup_heads(x):
    lane = lax.broadcasted_iota(jnp.int32, x.shape, 1)
    swapped = pltpu.roll(x, ATT_HD, 1)
    low = lane < ATT_HD
    return jnp.where(low, x, swapped), jnp.where(low, swapped, x)


def _attn_kernel(sink_ref, q_ref, k_ref, v_ref, cos_ref, sa_ref, sb_ref, pk_ref, pv_ref,
                 o_ref, knew_ref, vnew_ref, kd, vd, ktail, vtail, *, tq, has_past):
    j = pl.program_id(1)
    c = CHUNK
    span = WINDOW + c

    def store_rows(r0, n, k_rows, v_rows):
        k0, k1 = _dup_heads(k_rows)
        v0, v1 = _dup_heads(v_rows)
        ones = jnp.ones((n, LANES), BF16)
        kd[0, r0:r0 + n, :] = k0.astype(BF16)
        kd[1, r0:r0 + n, :] = k1.astype(BF16)
        vd[0, r0:r0 + n, :] = jnp.concatenate([v0.astype(BF16), ones], axis=1)
        vd[1, r0:r0 + n, :] = jnp.concatenate([v1.astype(BF16), ones], axis=1)

    @pl.when(j == 0)
    def _():
        store_rows(0, WINDOW, pk_ref[...], pv_ref[...])
        ktail[...] = pk_ref[...]
        vtail[...] = pv_ref[...]

    cos_f, sin_a, sin_b = cos_ref[...], sa_ref[...], sb_ref[...]
    kr = _rope_small(k_ref[...], cos_f, sin_a, sin_b)
    vr = v_ref[...]
    store_rows(WINDOW, tq, kr, vr)
    if tq >= WINDOW:
        ktail[...] = kr[tq - WINDOW:, :]
        vtail[...] = vr[tq - WINDOW:, :]
    else:
        ktail[...] = jnp.concatenate([ktail[tq:, :], kr], axis=0)
        vtail[...] = jnp.concatenate([vtail[tq:, :], vr], axis=0)

    rep = Q_W // KV_W
    qr = _rope_small(q_ref[...], jnp.concatenate([cos_f] * rep, axis=1), jnp.concatenate([sin_a] * rep, axis=1),
                     jnp.concatenate([sin_b] * rep, axis=1)) * (ATT_HD ** -0.5)
    low = lax.broadcasted_iota(jnp.int32, (c, LANES), 1) < ATT_HD
    scol = lax.broadcasted_iota(jnp.int32, (ATT_GROUP * c, span), 1)
    srow_head = lax.shift_right_logical(lax.broadcasted_iota(jnp.int32, (ATT_GROUP * c, 1), 0),
                                        jnp.int32(c.bit_length() - 1))
    units = [(i, kvh) for i in range(tq // c) for kvh in range(ATT_KV_HEADS)]
    sks = []
    for kvh in range(ATT_KV_HEADS):
        sk = jnp.zeros((ATT_GROUP * c, 1), F32)
        for gq in range(ATT_GROUP):
            sk = jnp.where(srow_head == gq, sink_ref[kvh * ATT_GROUP + gq], sk)
        sks.append(sk)
    scores = []
    for i, kvh in units:
        stack = []
        for gq in range(ATT_GROUP):
            h = kvh * ATT_GROUP + gq
            pair = qr[i * c:(i + 1) * c, (h // 2) * LANES:(h // 2 + 1) * LANES]
            stack.append(jnp.where(low, pair, 0.0) if h % 2 == 0 else jnp.where(low, 0.0, pair))
        q_stack = jnp.concatenate(stack, axis=0).astype(BF16)
        s = lax.dot_general(q_stack, kd[kvh, i * c:i * c + span, :], NT, preferred_element_type=F32)
        if not has_past:
            blk = j * (tq // c) + i
            s = jnp.where(blk * c + scol < WINDOW, NEG_INF, s)
        scores.append(s)
    maxes = [jnp.maximum(jnp.max(s, axis=-1, keepdims=True), sks[kvh]) for s, (i, kvh) in zip(scores, units)]
    exps = [jnp.exp(s - m).astype(BF16) for s, m in zip(scores, maxes)]
    outs = [lax.dot_general(e, vd[kvh, i * c:i * c + span, :], NN, preferred_element_type=F32)
            for e, (i, kvh) in zip(exps, units)]
    for oe, m, (i, kvh) in zip(outs, maxes, units):
        o = oe[:, :LANES] / (oe[:, LANES:] + jnp.exp(sks[kvh] - m))
        for t in range(ATT_GROUP // 2):
            h = kvh * ATT_GROUP + 2 * t
            pair_out = jnp.where(low, o[2 * t * c:(2 * t + 1) * c], o[(2 * t + 1) * c:(2 * t + 2) * c])
            o_ref[i * c:(i + 1) * c, (h // 2) * LANES:(h // 2 + 1) * LANES] = pair_out
    kd[:, 0:WINDOW, :] = kd[:, tq:tq + WINDOW, :]
    vd[:, 0:WINDOW, :] = vd[:, tq:tq + WINDOW, :]

    @pl.when(j == pl.num_programs(1) - 1)
    def _():
        knew_ref[...] = ktail[...]
        vnew_ref[...] = vtail[...]


def _attention(qkv, cos_f, sin_a, sin_b, past_k, past_v, sink, tq, has_past):
    bsz, seq, _ = qkv.shape
    nq = Q_W // KV_W
    tab = pl.BlockSpec((tq, KV_W), lambda b, j: (j, 0))
    past = pl.BlockSpec((None, WINDOW, KV_W), lambda b, j: (b, 0, 0))
    return pl.pallas_call(
        functools.partial(_attn_kernel, tq=tq, has_past=has_past),
        grid=(bsz, seq // tq),
        in_specs=[
            pl.BlockSpec(memory_space=pltpu.SMEM),
            pl.BlockSpec((None, tq, Q_W), lambda b, j: (b, j, 0)),
            pl.BlockSpec((None, tq, KV_W), lambda b, j: (b, j, nq)),
            pl.BlockSpec((None, tq, KV_W), lambda b, j: (b, j, nq + 1)),
            tab, tab, tab, past, past,
        ],
        out_specs=[pl.BlockSpec((None, tq, Q_W), lambda b, j: (b, j, 0)), past, past],
        out_shape=[
            jax.ShapeDtypeStruct((bsz, seq, Q_W), F32),
            jax.ShapeDtypeStruct((bsz, WINDOW, KV_W), F32),
            jax.ShapeDtypeStruct((bsz, WINDOW, KV_W), F32),
        ],
        scratch_shapes=[
            pltpu.VMEM((ATT_KV_HEADS, WINDOW + tq, LANES), BF16),
            pltpu.VMEM((ATT_KV_HEADS, WINDOW + tq, 2 * LANES), BF16),
            pltpu.VMEM((WINDOW, KV_W), F32),
            pltpu.VMEM((WINDOW, KV_W), F32),
        ],
        compiler_params=_params(("parallel", "arbitrary")),
        name="swa_attention",
    )(sink, qkv, qkv, qkv, cos_f, sin_a, sin_b, past_k, past_v)


def _rope_tables(pos0, seq, rot_dim, theta):
    half = rot_dim // 2
    inv = np.exp(np.arange(half, dtype=np.float64) * (-2.0 * math.log(theta) / rot_dim))
    ang = np.arange(pos0, pos0 + seq).astype(np.float64)[:, None] * inv[None, :]
    return jnp.asarray(np.cos(ang), F32), jnp.asarray(np.sin(ang), F32)


def _largest_tile(n, cap):
    t = cap
    while n % t:
        t //= 2
    return t


def _prepare(prm):
    w = RWKV_WIDTH
    qkvg_w = 2 * RET_HEADS * RET_DK + 2 * RET_WIDTH
    cu_w = 2 * CONV_CH
    pp = {}
    pp["w_in_ab_a"] = prm["w_in_ab"][0][:, :qkvg_w].astype(BF16)
    pp["w_in_ab_b"] = prm["w_in_ab"][0][:, qkvg_w:].astype(BF16)
    pp["w_out_ab_a"] = prm["w_out_ab"][0][:RET_WIDTH].astype(BF16)
    pp["w_out_ab_b"] = prm["w_out_ab"][0][RET_WIDTH:].astype(BF16)
    pp["w_in_cd_a"] = prm["w_in_cd"][0][:, :cu_w].astype(BF16)
    pp["w_in_cd_b"] = prm["w_in_cd"][0][:, cu_w:].astype(BF16)
    pp["w_out_cd_a"] = prm["w_out_cd"][0][:CONV_CH].astype(BF16)
    pp["w_out_cd_b"] = prm["w_out_cd"][0][CONV_CH:].astype(BF16)
    for name in ("ffn_gate", "ffn_up", "ffn_down"):
        pp[name] = prm[name].astype(BF16)
    zeros = jnp.zeros((RWKV_W_LORA, w), F32)
    pp["rwkv_wwa"] = jnp.concatenate(
        [jnp.concatenate([prm["rwkv_w2"][0], zeros], axis=1), jnp.concatenate([zeros, prm["rwkv_a2"][0]], axis=1)],
        axis=0).astype(BF16)
    pp["rwkv_g2"] = prm["rwkv_g2"][0].astype(BF16)
    pp["rwkv_vecs"] = jnp.stack(
        [prm["rwkv_w0"][0], prm["rwkv_a0"][0], prm["rwkv_kk"][0], prm["rwkv_ka"][0], prm["rwkv_rk"][0].reshape(w),
         prm["rwkv_ln_g"][0], prm["rwkv_ln_b"][0], jnp.zeros((w,), F32)], axis=0)
    pp["rwkv_mu"] = prm["rwkv_mu"][0].reshape(1, RWKV_PROJ)
    head = np.arange(GROUP_LANES) // RWKV_HD
    pp["bdm"] = jnp.asarray(head[:, None] == head[None, :], BF16)
    pp["conv_w"] = jnp.concatenate([prm["conv_w"][0], jnp.zeros((CONV_PAD - CONV_WIDTH, CONV_CH), F32)], axis=0)
    pp["conv_vecs"] = jnp.concatenate(
        [jnp.stack([prm["conv_b"][0], prm["conv_ln_g"][0], prm["conv_ln_b"][0]], axis=0),
         jnp.zeros((5, CONV_CH), F32)], axis=0)
    return pp


def _rwkv_tiling(bsz, seq):
    tcb = _largest_tile(seq, RWKV_CHUNKS_PER_STEP * CHUNK)
    bb = _largest_tile(bsz, max(1, (RWKV_CHUNKS_PER_STEP * CHUNK) // tcb))
    return bb, tcb


def _trunk(x, pos0, has_past, st_ret, st_wkv, st_shift, st_conv, st_k, st_v, prm, pp):
    bsz, seq, _ = x.shape
    tokens = bsz * seq
    tm = _largest_tile(tokens, 256)
    row = lambda v: v.reshape(1, -1)

    x2 = x.reshape(tokens, D_MODEL)
    qkvg, p = _in_proj(x2, row(prm["norm_mix"][0]), pp["w_in_ab_a"], pp["w_in_ab_b"], tm)
    qkvg = qkvg.reshape(bsz, seq, -1)
    p = p.reshape(bsz, seq, RWKV_PROJ)
    cos, sin = _rope_tables(pos0, seq, RET_DK, RET_THETA)
    cos_t = jnp.concatenate([cos, cos], axis=1)
    sin_t = jnp.concatenate([-sin, sin], axis=1)
    o_ret, ret_new = _retention(qkvg, cos_t, sin_t, row(prm["ret_gn_g"][0]), st_ret, _largest_tile(seq, 256))
    h0 = st_wkv.transpose(0, 3, 1, 2).reshape(bsz, RWKV_HD, RWKV_WIDTH)
    bb, tcb = _rwkv_tiling(bsz, seq)
    o_rwkv, h_new = _rwkv(p, st_shift.reshape(bsz, 1, RWKV_PROJ), h0, pp["rwkv_mu"], pp["rwkv_vecs"],
                          pp["rwkv_wwa"], pp["rwkv_g2"], pp["bdm"], bb, tcb)
    wkv_new = h_new.reshape(bsz, RWKV_HD, RWKV_HEADS, RWKV_HD).transpose(0, 2, 3, 1)
    shift_new = p[:, -1]
    x2 = _post(x2, o_ret.reshape(tokens, -1), o_rwkv.reshape(tokens, -1), pp["w_out_ab_a"], pp["w_out_ab_b"],
               row(prm["norm_ffn"][0]), pp["ffn_gate"][0], pp["ffn_up"][0], pp["ffn_down"][0],
               row(prm["norm_final"]), tm, final=False)

    cu, qkv = _in_proj(x2, row(prm["norm_mix"][1]), pp["w_in_cd_a"], pp["w_in_cd_b"], tm)
    cu = cu.reshape(bsz, seq, -1)
    qkv = qkv.reshape(bsz, seq, -1)
    buf_pad = jnp.concatenate([jnp.zeros((bsz, CONV_PAD - (CONV_WIDTH - 1), CONV_CH), F32), st_conv], axis=1)
    o_conv, conv_new = _conv(cu, buf_pad, pp["conv_w"], pp["conv_vecs"], _largest_tile(seq, 256))
    cos, sin = _rope_tables(pos0, seq, ROT_DIM, ROPE_THETA)
    half = ROT_DIM // 2
    pad = jnp.zeros((seq, ATT_HD - ROT_DIM), F32)
    z8 = jnp.zeros((seq, half), F32)
    cos_f = jnp.concatenate([cos, cos, pad + 1.0], axis=1)
    sin_a = jnp.concatenate([-sin, z8, pad], axis=1)
    sin_b = jnp.concatenate([z8, sin, pad], axis=1)
    tile2 = lambda t: jnp.concatenate([t] * ATT_KV_HEADS, axis=1)
    o_att, k_new, v_new = _attention(qkv, tile2(cos_f), tile2(sin_a), tile2(sin_b),
                                     st_k.reshape(bsz, WINDOW, KV_W), st_v.reshape(bsz, WINDOW, KV_W),
                                     prm["attn_sink"][0], _largest_tile(seq, 256), has_past)
    y2 = _post(x2, o_conv.reshape(tokens, -1), o_att.reshape(tokens, -1), pp["w_out_cd_a"], pp["w_out_cd_b"],
               row(prm["norm_ffn"][1]), pp["ffn_gate"][1], pp["ffn_up"][1], pp["ffn_down"][1],
               row(prm["norm_final"]), tm, final=True)
    kv_shape = (bsz, WINDOW, ATT_KV_HEADS, ATT_HD)
    return (y2.reshape(bsz, seq, D_MODEL), ret_new[None], wkv_new[None], shift_new[None], conv_new[None],
            k_new.reshape(kv_shape)[None], v_new.reshape(kv_shape)[None])


def kernel(x_prompt, x_sample, state_ret, state_wkv, state_shift, state_conv, cache_k, cache_v, norm_mix, norm_ffn, norm_final, w_in_ab, w_out_ab, ret_gn_g, rwkv_mu, rwkv_w0, rwkv_w2, rwkv_a0, rwkv_a2, rwkv_g2, rwkv_kk, rwkv_ka, rwkv_rk, rwkv_ln_g, rwkv_ln_b, w_in_cd, w_out_cd, conv_w, conv_b, conv_ln_g, conv_ln_b, attn_sink, ffn_gate, ffn_up, ffn_down):
    prm = {
        'norm_mix': norm_mix, 'norm_ffn': norm_ffn, 'norm_final': norm_final,
        'w_in_ab': w_in_ab, 'w_out_ab': w_out_ab, 'ret_gn_g': ret_gn_g,
        'rwkv_mu': rwkv_mu, 'rwkv_w0': rwkv_w0, 'rwkv_w2': rwkv_w2, 'rwkv_a0': rwkv_a0, 'rwkv_a2': rwkv_a2,
        'rwkv_g2': rwkv_g2, 'rwkv_kk': rwkv_kk, 'rwkv_ka': rwkv_ka, 'rwkv_rk': rwkv_rk,
        'rwkv_ln_g': rwkv_ln_g, 'rwkv_ln_b': rwkv_ln_b,
        'w_in_cd': w_in_cd, 'w_out_cd': w_out_cd, 'conv_w': conv_w, 'conv_b': conv_b,
        'conv_ln_g': conv_ln_g, 'conv_ln_b': conv_ln_b, 'attn_sink': attn_sink,
        'ffn_gate': ffn_gate, 'ffn_up': ffn_up, 'ffn_down': ffn_down,
    }
    pp = _prepare(prm)
    bsz = x_prompt.shape[0]
    dt = x_prompt.dtype
    z_ret = jnp.zeros((bsz, RET_HEADS, RET_DK, RET_DV), dt)
    z_wkv = jnp.zeros((bsz, RWKV_HEADS, RWKV_HD, RWKV_HD), dt)
    z_shift = jnp.zeros((bsz, RWKV_PROJ), dt)
    z_conv = jnp.zeros((bsz, CONV_WIDTH - 1, CONV_CH), dt)
    z_kv = jnp.zeros((bsz, WINDOW, ATT_KV_HEADS, ATT_HD), dt)
    out_p = _trunk(x_prompt, 0, False, z_ret, z_wkv, z_shift, z_conv, z_kv, z_kv, prm, pp)
    out_s = _trunk(x_sample, PAST_LEN, True, state_ret[0], state_wkv[0], state_shift[0], state_conv[0],
                   cache_k[0], cache_v[0], prm, pp)
    return (out_p[0], out_s[0]) + tuple(out_p[1:]) + tuple(out_s[1:])
```

```python
import functools
import math

import jax
import jax.numpy as jnp
import numpy as np
from jax import lax
from jax.experimental import pallas as pl
from jax.experimental.pallas import tpu as pltpu

F32 = jnp.float32
BF16 = jnp.bfloat16

D_MODEL = 1024
CHUNK = 64
RMS_EPS = 1e-6
LN_EPS = 1e-5

RET_HEADS = 4
RET_DK = 128
RET_DV = 128
RET_WIDTH = RET_HEADS * RET_DV
RET_THETA = 10000.0

RWKV_HEADS = 8
RWKV_HD = 64
RWKV_WIDTH = RWKV_HEADS * RWKV_HD
RWKV_W_LORA = 64
RWKV_A_LORA = 64
RWKV_WA_LORA = RWKV_W_LORA + RWKV_A_LORA
RWKV_G_LORA = 128
RWKV_PROJ = 3 * RWKV_WIDTH + RWKV_WA_LORA + RWKV_G_LORA
RWKV_LN_EPS = 64e-5

CONV_CH = 512
CONV_WIDTH = 31
CONV_PAD = 32
SUBLANES = 8

ATT_HEADS = 8
ATT_KV_HEADS = 2
ATT_HD = 64
ATT_GROUP = ATT_HEADS // ATT_KV_HEADS
WINDOW = 128
ROT_DIM = ATT_HD // 4
ROPE_THETA = 500000.0
NEG_INF = -1e30
LANES = 128

PAST_LEN = 4096

V7X_VMEM_LIMIT_BYTES = 56 * 1024 * 1024

NN = (((1,), (0,)), ((), ()))
NT = (((1,), (1,)), ((), ()))
TN = (((0,), (0,)), ((), ()))


def _split_bf16(x, n):
    pieces = []
    rem = x
    for i in range(n):
        piece = rem.astype(BF16)
        pieces.append(piece)
        if i + 1 < n:
            rem = rem - piece.astype(F32)
    return pieces


def _mmp(pa, pb, dn=NN):
    n = max(len(pa), len(pb))
    acc = None
    for i, x in enumerate(pa):
        for j, y in enumerate(pb):
            if i + j < n:
                t = lax.dot_general(x, y, dn, preferred_element_type=F32)
                acc = t if acc is None else acc + t
    return acc


def _mm(a, b, dn=NN, na=1, nb=1):
    return _mmp(_split_bf16(a, na), _split_bf16(b, nb), dn)


def _rms(x, g):
    ms = jnp.mean(x * x, axis=-1, keepdims=True)
    return (x * lax.rsqrt(ms + RMS_EPS)) * g


def _sigmoid(x):
    return 1.0 / (1.0 + jnp.exp(-x))


def _params(sem):
    return pltpu.CompilerParams(dimension_semantics=sem, vmem_limit_bytes=V7X_VMEM_LIMIT_BYTES)


def _const_spec(shape):
    nd = len(shape)
    return pl.BlockSpec(shape, lambda *_: (0,) * nd, pipeline_mode=pl.Buffered(1))


def _in_proj_kernel(x_ref, g_ref, wa_ref, wb_ref, oa_ref, ob_ref):
    h = _rms(x_ref[...], g_ref[...]).astype(BF16)
    oa_ref[...] = jnp.dot(h, wa_ref[...], preferred_element_type=F32)
    ob_ref[...] = jnp.dot(h, wb_ref[...], preferred_element_type=F32)


def _in_proj(x2, g, wa, wb, tm):
    t = x2.shape[0]
    na, nb = wa.shape[1], wb.shape[1]
    return pl.pallas_call(
        _in_proj_kernel,
        grid=(t // tm,),
        in_specs=[
            pl.BlockSpec((tm, D_MODEL), lambda i: (i, 0)),
            _const_spec((1, D_MODEL)),
            _const_spec((D_MODEL, na)),
            _const_spec((D_MODEL, nb)),
        ],
        out_specs=[
            pl.BlockSpec((tm, na), lambda i: (i, 0)),
            pl.BlockSpec((tm, nb), lambda i: (i, 0)),
        ],
        out_shape=[jax.ShapeDtypeStruct((t, na), F32), jax.ShapeDtypeStruct((t, nb), F32)],
        compiler_params=_params(("parallel",)),
        name="in_proj",
    )(x2, g, wa, wb)


def _post_kernel(x_ref, a_ref, b_ref, wa_ref, wb_ref, g_ref, wg_ref, wu_ref, wd_ref, gf_ref, o_ref, *, final):
    x = x_ref[...]
    x = x + jnp.dot(a_ref[...].astype(BF16), wa_ref[...], preferred_element_type=F32)
    x = x + jnp.dot(b_ref[...].astype(BF16), wb_ref[...], preferred_element_type=F32)
    h = _rms(x, g_ref[...]).astype(BF16)
    gate = jnp.dot(h, wg_ref[...], preferred_element_type=F32)
    up = jnp.dot(h, wu_ref[...], preferred_element_type=F32)
    act = (gate * _sigmoid(gate)) * up
    x = x + jnp.dot(act.astype(BF16), wd_ref[...], preferred_element_type=F32)
    if final:
        x = _rms(x, gf_ref[...])
    o_ref[...] = x


def _post(x2, a2, b2, wa, wb, g, wg, wu, wd, gf, tm, final):
    t = x2.shape[0]
    ka, kb, dff = a2.shape[1], b2.shape[1], wg.shape[1]
    return pl.pallas_call(
        functools.partial(_post_kernel, final=final),
        grid=(t // tm,),
        in_specs=[
            pl.BlockSpec((tm, D_MODEL), lambda i: (i, 0)),
            pl.BlockSpec((tm, ka), lambda i: (i, 0)),
            pl.BlockSpec((tm, kb), lambda i: (i, 0)),
            _const_spec((ka, D_MODEL)),
            _const_spec((kb, D_MODEL)),
            _const_spec((1, D_MODEL)),
            _const_spec((D_MODEL, dff)),
            _const_spec((D_MODEL, dff)),
            _const_spec((dff, D_MODEL)),
            _const_spec((1, D_MODEL)),
        ],
        out_specs=pl.BlockSpec((tm, D_MODEL), lambda i: (i, 0)),
        out_shape=jax.ShapeDtypeStruct((t, D_MODEL), F32),
        compiler_params=_params(("parallel",)),
        name="post_final" if final else "post",
    )(x2, a2, b2, wa, wb, g, wg, wu, wd, gf)


def _ret_kernel(cdec_ref, q_ref, k_ref, v_ref, gt_ref, cos_ref, sin_ref, dmat_ref, qdec_ref, kdec_ref, gn_ref, r0_ref,
                o_ref, rfin_ref, r_scr):
    j = pl.program_id(1)

    @pl.when(j == 0)
    def _():
        r_scr[...] = r0_ref[...]

    cos = cos_ref[...]
    sin = sin_ref[...]
    for h in range(RET_HEADS):
        sl = slice(h * RET_DK, (h + 1) * RET_DK)
        q = q_ref[:, sl]
        k = k_ref[:, sl]
        v = v_ref[:, sl]
        qr = q * cos + pltpu.roll(q, RET_DK // 2, 1) * sin
        kr = (k * cos + pltpu.roll(k, RET_DK // 2, 1) * sin) * (RET_DK ** -0.5)
        inner = _mm(qr, kr, NT) * dmat_ref[h]
        r = r_scr[h]
        o = _mm(inner, v) + _mm(qr, r) * qdec_ref[h]
        r_scr[h] = r * cdec_ref[h] + _mm(kr * kdec_ref[h], v, TN)
        oc = o - jnp.mean(o, axis=-1, keepdims=True)
        y = oc * lax.rsqrt(jnp.mean(oc * oc, axis=-1, keepdims=True) + LN_EPS) * gn_ref[:, sl]
        gt = gt_ref[:, sl]
        o_ref[:, sl] = (gt * _sigmoid(gt)) * y

    @pl.when(j == pl.num_programs(1) - 1)
    def _():
        rfin_ref[...] = r_scr[...]


def _ret_tables(c):
    lg = np.log1p(-np.exp2(-5.0 - np.arange(RET_HEADS, dtype=np.float64)))
    idx = np.arange(c, dtype=np.float64)
    diff = idx[:, None] - idx[None, :]
    dmat = np.where(diff >= 0, np.exp(lg[:, None, None] * np.maximum(diff, 0.0)), 0.0)
    qdec = np.exp(lg[:, None] * (idx + 1.0))
    kdec = np.exp(lg[:, None] * (c - 1.0 - idx))
    qdec = np.broadcast_to(qdec[:, :, None], (RET_HEADS, c, RET_DV))
    kdec = np.broadcast_to(kdec[:, :, None], (RET_HEADS, c, RET_DK))
    cdec = np.exp(lg * c)
    return tuple(jnp.asarray(t, F32) for t in (dmat, qdec, kdec, cdec))


def _retention(qkvg, cos, sin, gn, r0, c):
    bsz, seq, _ = qkvg.shape
    dmat, qdec, kdec, cdec = _ret_tables(c)
    col = lambda n: pl.BlockSpec((None, c, RET_WIDTH), lambda b, j, n=n: (b, j, n))
    return pl.pallas_call(
        _ret_kernel,
        grid=(bsz, seq // c),
        in_specs=[
            pl.BlockSpec(memory_space=pltpu.SMEM),
            col(0), col(1), col(2), col(3),
            pl.BlockSpec((c, RET_DK), lambda b, j: (j, 0)),
            pl.BlockSpec((c, RET_DK), lambda b, j: (j, 0)),
            _const_spec((RET_HEADS, c, c)),
            _const_spec((RET_HEADS, c, RET_DV)),
            _const_spec((RET_HEADS, c, RET_DK)),
            _const_spec((1, RET_WIDTH)),
            pl.BlockSpec((None, RET_HEADS, RET_DK, RET_DV), lambda b, j: (b, 0, 0, 0)),
        ],
        out_specs=[
            pl.BlockSpec((None, c, RET_WIDTH), lambda b, j: (b, j, 0)),
            pl.BlockSpec((None, RET_HEADS, RET_DK, RET_DV), lambda b, j: (b, 0, 0, 0)),
        ],
        out_shape=[
            jax.ShapeDtypeStruct((bsz, seq, RET_WIDTH), F32),
            jax.ShapeDtypeStruct((bsz, RET_HEADS, RET_DK, RET_DV), F32),
        ],
        scratch_shapes=[pltpu.VMEM((RET_HEADS, RET_DK, RET_DV), F32)],
        compiler_params=_params(("parallel", "arbitrary")),
        name="retention",
    )(cdec, qkvg, qkvg, qkvg, qkvg, cos, sin, dmat, qdec, kdec, gn, r0)


GROUP_LANES = 256
GROUP_HEADS = GROUP_LANES // RWKV_HD
N_GROUPS = RWKV_WIDTH // GROUP_LANES
RWKV_SECTION_CHUNKS = 4
RWKV_CHUNKS_PER_STEP = 8

RWKV_PREC = {
    "seg": 1,
    "seg_bonus": 1,
    "seg_ln": 1,
    "cum": 2,
    "p12": (1, 1),
    "dbl": (1, 1),
    "av": (1, 1),
    "til": (1, 1),
    "wc": 2,
    "ur": (1, 1),
    "yu": (1, 1),
    "hu": (1, 1),
}


def _rwkv_kernel(p_ref, prev0_ref, h0_ref, mu_ref, vec_ref, wwa_ref, g2_ref, ltri_ref, bdm_ref,
                 o_ref, hfin_ref, h_scr, prev_scr, *, bb, tcb):
    j = pl.program_id(1)
    c = CHUNK
    w = RWKV_WIDTH
    gl = GROUP_LANES
    rows = bb * tcb
    prec = RWKV_PREC

    @pl.when(j == 0)
    def _():
        h_scr[...] = h0_ref[...]
        prev_scr[...] = prev0_ref[...]

    p = p_ref[...].reshape(rows, RWKV_PROJ)
    sec = min(rows, RWKV_SECTION_CHUNKS * c)
    n_sec = rows // sec
    n_chunks = tcb // c

    w0 = vec_ref[0:1, :]
    a0 = vec_ref[1:2, :]
    k_k = vec_ref[2:3, :]
    k_a = vec_ref[3:4, :]
    r_k = vec_ref[4:5, :]
    ln_g = vec_ref[5:6, :]
    ln_b = vec_ref[6:7, :]
    bdm = bdm_ref[...]

    def segsum(t, key="seg"):
        return jnp.concatenate(
            [_mmp(_split_bf16(t[:, gi * gl:(gi + 1) * gl], prec[key]), [bdm]) for gi in range(N_GROUPS)], axis=1)

    shift = jnp.int32(RWKV_HD.bit_length() - 1)
    lane_blk = lax.shift_right_logical(lax.broadcasted_iota(jnp.int32, (c, gl), 1), shift)
    trow = lax.broadcasted_iota(jnp.int32, (c, gl), 0)
    tcol = jnp.bitwise_and(lax.broadcasted_iota(jnp.int32, (c, gl), 1), jnp.int32(RWKV_HD - 1))
    strict = trow > tcol
    incl = trow >= tcol
    eye = jnp.where(trow == tcol, 1.0, 0.0).astype(F32)
    eye_b = eye.astype(BF16)

    def bd(pieces):
        return [jnp.concatenate([q] * GROUP_HEADS, axis=0) * bdm for q in pieces]

    def unbd(full):
        out = full[0:RWKV_HD]
        for hh in range(1, GROUP_HEADS):
            out = jnp.where(lane_blk == hh, full[hh * RWKV_HD:(hh + 1) * RWKV_HD], out)
        return out

    def prod(key, x, y, dn=NN, block_diag=True):
        na, nb = prec[key]
        pb = _split_bf16(y, nb)
        return _mmp(_split_bf16(x, na), bd(pb) if block_diag else pb, dn)

    h_state = [[h_scr[b, :, gi * gl:(gi + 1) * gl] for gi in range(N_GROUPS)] for b in range(bb)]

    def section(si):
        st = {}
        r0 = si * sec

        def token_shift():
            ps = p[r0:r0 + sec]
            row = lax.broadcasted_iota(jnp.int32, ps.shape, 0)
            prev = pltpu.roll(ps, 1, 0)
            if r0 % tcb:
                prev = jnp.where(row == 0, p[r0 - 1:r0], prev)
            for b in range(bb):
                if r0 <= b * tcb < r0 + sec:
                    prev = jnp.where(row == b * tcb - r0, prev_scr[b], prev)
            xs = ps + (prev - ps) * mu_ref[...]
            st["r"] = xs[:, 0:w]
            st["k"] = xs[:, w:2 * w]
            st["v"] = xs[:, 2 * w:3 * w]
            wa_in = xs[:, 3 * w:3 * w + RWKV_WA_LORA]
            lane = lax.broadcasted_iota(jnp.int32, wa_in.shape, 1)
            wa_in = jnp.where(lane < RWKV_W_LORA, jnp.tanh(wa_in), wa_in)
            st["wa"] = _mm(wa_in, wwa_ref[...])
            st["g"] = _mm(_sigmoid(xs[:, 3 * w + RWKV_WA_LORA:]), g2_ref[...])

        def gates():
            st["lw"] = -math.exp(-0.5) * _sigmoid(w0 + st["wa"][:, :w])
            st["a"] = _sigmoid(a0 + st["wa"][:, w:])
            st["cum"] = _mmp([ltri_ref[...]], _split_bf16(st["lw"], prec["cum"]))

        def keys():
            kk = st["k"] * k_k
            kkn = kk * lax.rsqrt(jnp.maximum(segsum(kk * kk), 1e-24))
            st["k2"] = st["k"] * (1.0 + (st["a"] - 1.0) * k_a)
            st["kkn"] = kkn
            st["b_gate"] = kkn * st["a"]

        def bonus():
            st["bonus"] = segsum(st["r"] * st["k2"] * r_k, "seg_bonus") * st["v"]

        def make_units():
            units = []
            for ci in range(sec // c):
                rs = slice(ci * c, (ci + 1) * c)
                cum_c = st["cum"][rs]
                cum_last = cum_c[c - 1:c, :]
                e_neg = jnp.exp(-cum_c)
                e_rel = jnp.exp(cum_last - cum_c)
                a_hat = -st["kkn"][rs] * jnp.exp(cum_c - st["lw"][rs])
                r_hat = st["r"][rs] * jnp.exp(cum_c)
                b_hat = st["b_gate"][rs] * e_neg
                k_hat = st["k2"][rs] * e_neg
                b_til = st["b_gate"][rs] * e_rel
                k_til = st["k2"][rs] * e_rel
                w_c = jnp.exp(cum_last)
                v_c = st["v"][rs]
                for gi in range(N_GROUPS):
                    sl = slice(gi * gl, (gi + 1) * gl)
                    units.append(dict(ah=a_hat[:, sl], rh=r_hat[:, sl], bh=b_hat[:, sl], kh=k_hat[:, sl],
                                      bt=b_til[:, sl], kt=k_til[:, sl], vg=v_c[:, sl], wc=w_c[:, sl]))
            for un in units:
                un["lhs"] = jnp.concatenate([un["ah"], un["rh"]], axis=0)
            st["units"] = units

        token_stages = [token_shift, gates, keys, bonus, make_units]

        def each_unit(fn):
            def stage():
                for un in st["units"]:
                    fn(un)
            return stage

        def s_p1(un):
            p1 = prod("p12", un["lhs"], un["bh"], NT)
            un["a_ab"] = jnp.where(strict, p1[:c], 0.0)
            un["p_rb"] = jnp.where(incl, p1[c:], 0.0)

        def s_p2(un):
            p2 = prod("p12", un["lhs"], un["kh"], NT)
            un["a_ak"] = jnp.where(strict, p2[:c], 0.0)
            un["p_rk"] = jnp.where(incl, p2[c:], 0.0)

        def s_square(un):
            un["t_inv"] = eye + un["a_ab"]
            un["pw"] = prod("dbl", un["a_ab"], un["a_ab"])

        def s_double(un):
            tp = prod("dbl", jnp.concatenate([un["t_inv"], un["pw"]], axis=0), un["pw"])
            un["t_inv"] = un["t_inv"] + tp[:c]
            un["pw"] = tp[c:]

        def s_last(un):
            un["t_inv"] = un["t_inv"] + prod("dbl", un["t_inv"], un["pw"])

        def s_av(un):
            un["av"] = prod("av", jnp.concatenate([un["a_ak"], un["p_rk"]], axis=0), un["vg"])

        def s_decay(un):
            zz = [jnp.broadcast_to(q, (gl, gl)) * bdm for q in _split_bf16(un["wc"], prec["wc"])]
            un["wc_col"] = _mmp([eye_b], zz, NT)
            un["btkt"] = jnp.concatenate([un["bt"], un["kt"]], axis=0)

        def s_atil(un):
            un["a_til"] = prod("til", un["t_inv"], un["ah"])

        def s_vtil(un):
            un["v_til"] = prod("til", un["t_inv"], un["av"][:c])

        chunk_stages = [each_unit(f) for f in
                        [s_p1, s_p2, s_square] + [s_double] * (int(math.log2(c)) - 2)
                        + [s_last, s_av, s_decay, s_atil, s_vtil]]

        st["y"] = []

        def state_stage(ci):
            def stage():
                flat = r0 // c + ci
                b = flat // n_chunks
                ys = []
                for gi in range(N_GROUPS):
                    un = st["units"][ci * N_GROUPS + gi]
                    h_old = h_state[b][gi]
                    ur = prod("ur", jnp.concatenate([un["a_til"], un["rh"]], axis=0), h_old)
                    u = ur[:c] + un["v_til"]
                    ys.append(ur[c:] + prod("yu", un["p_rb"], u) + un["av"][c:])
                    upd = prod("hu", un["btkt"], jnp.concatenate([u, un["vg"]], axis=0), TN, block_diag=False)
                    h_state[b][gi] = un["wc_col"] * h_old + unbd(upd)
                st["y"].append(jnp.concatenate(ys, axis=1))
            return stage

        def normalise():
            y = jnp.concatenate(st["y"], axis=0)
            mean = segsum(y, "seg_ln") * (1.0 / RWKV_HD)
            yc = y - mean
            var = segsum(yc * yc, "seg_ln") * (1.0 / RWKV_HD)
            yn = yc * lax.rsqrt(var + RWKV_LN_EPS) * ln_g + ln_b
            out = (yn + st["bonus"]) * st["g"]
            if tcb >= sec:
                o_ref[r0 // tcb, r0 % tcb:r0 % tcb + sec, :] = out
            else:
                o_ref[r0 // tcb:(r0 + sec) // tcb, :, :] = out.reshape(sec // tcb, tcb, w)

        serial_stages = [state_stage(ci) for ci in range(sec // c)] + [normalise]
        return token_stages, chunk_stages, serial_stages

    def emit(*stage_lists):
        lists = [l for l in stage_lists if l]
        n = max(len(l) for l in lists)
        done = [0] * len(lists)
        for step in range(1, n + 1):
            for li, l in enumerate(lists):
                target = (step * len(l) + n - 1) // n
                while done[li] < target:
                    l[done[li]]()
                    done[li] += 1

    secs = [section(si) for si in range(n_sec)]
    emit(secs[0][0])
    for si in range(n_sec):
        emit(secs[si][1], secs[si + 1][0] if si + 1 < n_sec else [], secs[si - 1][2] if si else [])
    emit(secs[n_sec - 1][2])

    for b in range(bb):
        prev_scr[b] = p[(b + 1) * tcb - 1:(b + 1) * tcb, :]
        for gi in range(N_GROUPS):
            h_scr[b, :, gi * gl:(gi + 1) * gl] = h_state[b][gi]

    @pl.when(j == pl.num_programs(1) - 1)
    def _():
        hfin_ref[...] = h_scr[...]


def _rwkv(p, prev0, h0, mu, vecs, wwa, g2, bdm, bb, tcb):
    bsz, seq, _ = p.shape
    sec = min(bb * tcb, RWKV_SECTION_CHUNKS * CHUNK)
    t = np.arange(sec)
    ltri = jnp.asarray((t[:, None] >= t[None, :]) & (t[:, None] // CHUNK == t[None, :] // CHUNK), BF16)
    return pl.pallas_call(
        functools.partial(_rwkv_kernel, bb=bb, tcb=tcb),
        grid=(bsz // bb, seq // tcb),
        in_specs=[
            pl.BlockSpec((bb, tcb, RWKV_PROJ), lambda b, j: (b, j, 0)),
            pl.BlockSpec((bb, 1, RWKV_PROJ), lambda b, j: (b, 0, 0)),
            pl.BlockSpec((bb, RWKV_HD, RWKV_WIDTH), lambda b, j: (b, 0, 0)),
            _const_spec((1, RWKV_PROJ)),
            _const_spec((8, RWKV_WIDTH)),
            _const_spec((RWKV_WA_LORA, 2 * RWKV_WIDTH)),
            _const_spec((RWKV_G_LORA, RWKV_WIDTH)),
            _const_spec((sec, sec)),
            _const_spec((GROUP_LANES, GROUP_LANES)),
        ],
        out_specs=[
            pl.BlockSpec((bb, tcb, RWKV_WIDTH), lambda b, j: (b, j, 0)),
            pl.BlockSpec((bb, RWKV_HD, RWKV_WIDTH), lambda b, j: (b, 0, 0)),
        ],
        out_shape=[
            jax.ShapeDtypeStruct((bsz, seq, RWKV_WIDTH), F32),
            jax.ShapeDtypeStruct((bsz, RWKV_HD, RWKV_WIDTH), F32),
        ],
        scratch_shapes=[pltpu.VMEM((bb, RWKV_HD, RWKV_WIDTH), F32), pltpu.VMEM((bb, 1, RWKV_PROJ), F32)],
        compiler_params=_params(("parallel", "arbitrary")),
        name="rwkv7",
    )(p, prev0, h0, mu, vecs, wwa, g2, ltri, bdm)


def _conv_kernel(cu_ref, buf_ref, w_ref, vec_ref, o_ref, bufout_ref, hist, *, tm):
    j = pl.program_id(1)

    @pl.when(j == 0)
    def _():
        hist[0:CONV_PAD, :] = buf_ref[...]

    cu = cu_ref[...]
    hist[CONV_PAD:CONV_PAD + tm, :] = cu[:, :CONV_CH] * _sigmoid(cu[:, CONV_CH:])
    full = hist[...]
    n_hist = tm + CONV_PAD
    off = CONV_PAD - (CONV_WIDTH - 1)
    acc = jnp.broadcast_to(vec_ref[0:1, :], (tm, CONV_CH))
    for res in range(SUBLANES):
        shifted = full if res == 0 else pltpu.roll(full, n_hist - res, 0)
        for i in range(CONV_WIDTH):
            if (off + i) % SUBLANES == res:
                base = off + i - res
                acc = acc + shifted[base:base + tm, :] * w_ref[i:i + 1, :]
    yc = acc - jnp.mean(acc, axis=-1, keepdims=True)
    y = yc * lax.rsqrt(jnp.mean(yc * yc, axis=-1, keepdims=True) + LN_EPS) * vec_ref[1:2, :] + vec_ref[2:3, :]
    o_ref[...] = y * _sigmoid(y)
    tail = full[tm:tm + CONV_PAD, :]
    hist[0:CONV_PAD, :] = tail

    @pl.when(j == pl.num_programs(1) - 1)
    def _():
        bufout_ref[...] = tail[off:, :]


def _conv(cu, buf_pad, conv_w, vecs, tm):
    bsz, seq, _ = cu.shape
    return pl.pallas_call(
        functools.partial(_conv_kernel, tm=tm),
        grid=(bsz, seq // tm),
        in_specs=[
            pl.BlockSpec((None, tm, 2 * CONV_CH), lambda b, j: (b, j, 0)),
            pl.BlockSpec((None, CONV_PAD, CONV_CH), lambda b, j: (b, 0, 0)),
            _const_spec((CONV_PAD, CONV_CH)),
            _const_spec((8, CONV_CH)),
        ],
        out_specs=[
            pl.BlockSpec((None, tm, CONV_CH), lambda b, j: (b, j, 0)),
            pl.BlockSpec((None, CONV_WIDTH - 1, CONV_CH), lambda b, j: (b, 0, 0)),
        ],
        out_shape=[
            jax.ShapeDtypeStruct((bsz, seq, CONV_CH), F32),
            jax.ShapeDtypeStruct((bsz, CONV_WIDTH - 1, CONV_CH), F32),
        ],
        scratch_shapes=[pltpu.VMEM((tm + CONV_PAD, CONV_CH), F32)],
        compiler_params=_params(("parallel", "arbitrary")),
        name="conv_module",
    )(cu, buf_pad, conv_w, vecs)


KV_W = ATT_KV_HEADS * ATT_HD
Q_W = ATT_HEADS * ATT_HD
assert KV_W == LANES


def _rope_small(x, cos_f, sin_a, sin_b):
    n = x.shape[1]
    return x * cos_f + pltpu.roll(x, n - ROT_DIM // 2, 1) * sin_a + pltpu.roll(x, ROT_DIM // 2, 1) * sin_b


def _dup_heads(x):
    lane = lax.broadcasted_iota(jnp.int32, x.shape, 1)
    swapped = pltpu.roll(x, ATT_HD, 1)
    low = lane < ATT_HD
    return jnp.where(low, x, swapped), jnp.where(low, swapped, x)


def _attn_kernel(sink_ref, q_ref, k_ref, v_ref, cos_ref, sa_ref, sb_ref, pk_ref, pv_ref,
                 o_ref, knew_ref, vnew_ref, kd, vd, ktail, vtail, *, tq, has_past):
    j = pl.program_id(1)
    c = CHUNK
    span = WINDOW + c

    def store_rows(r0, n, k_rows, v_rows):
        k0, k1 = _dup_heads(k_rows)
        v0, v1 = _dup_heads(v_rows)
        ones = jnp.ones((n, LANES), BF16)
        kd[0, r0:r0 + n, :] = k0.astype(BF16)
        kd[1, r0:r0 + n, :] = k1.astype(BF16)
        vd[0, r0:r0 + n, :] = jnp.concatenate([v0.astype(BF16), ones], axis=1)
        vd[1, r0:r0 + n, :] = jnp.concatenate([v1.astype(BF16), ones], axis=1)

    @pl.when(j == 0)
    def _():
        store_rows(0, WINDOW, pk_ref[...], pv_ref[...])
        ktail[...] = pk_ref[...]
        vtail[...] = pv_ref[...]

    cos_f, sin_a, sin_b = cos_ref[...], sa_ref[...], sb_ref[...]
    kr = _rope_small(k_ref[...], cos_f, sin_a, sin_b)
    vr = v_ref[...]
    store_rows(WINDOW, tq, kr, vr)
    if tq >= WINDOW:
        ktail[...] = kr[tq - WINDOW:, :]
        vtail[...] = vr[tq - WINDOW:, :]
    else:
        ktail[...] = jnp.concatenate([ktail[tq:, :], kr], axis=0)
        vtail[...] = jnp.concatenate([vtail[tq:, :], vr], axis=0)

    rep = Q_W // KV_W
    qr = _rope_small(q_ref[...], jnp.concatenate([cos_f] * rep, axis=1), jnp.concatenate([sin_a] * rep, axis=1),
                     jnp.concatenate([sin_b] * rep, axis=1)) * (ATT_HD ** -0.5)
    low = lax.broadcasted_iota(jnp.int32, (c, LANES), 1) < ATT_HD
    scol = lax.broadcasted_iota(jnp.int32, (ATT_GROUP * c, span), 1)
    srow_head = lax.shift_right_logical(lax.broadcasted_iota(jnp.int32, (ATT_GROUP * c, 1), 0),
                                        jnp.int32(c.bit_length() - 1))
    units = [(i, kvh) for i in range(tq // c) for kvh in range(ATT_KV_HEADS)]
    sks = []
    for kvh in range(ATT_KV_HEADS):
        sk = jnp.zeros((ATT_GROUP * c, 1), F32)
        for gq in range(ATT_GROUP):
            sk = jnp.where(srow_head == gq, sink_ref[kvh * ATT_GROUP + gq], sk)
        sks.append(sk)
    scores = []
    for i, kvh in units:
        stack = []
        for gq in range(ATT_GROUP):
            h = kvh * ATT_GROUP + gq
            pair = qr[i * c:(i + 1) * c, (h // 2) * LANES:(h // 2 + 1) * LANES]
            stack.append(jnp.where(low, pair, 0.0) if h % 2 == 0 else jnp.where(low, 0.0, pair))
        q_stack = jnp.concatenate(stack, axis=0).astype(BF16)
        s = lax.dot_general(q_stack, kd[kvh, i * c:i * c + span, :], NT, preferred_element_type=F32)
        if not has_past:
            blk = j * (tq // c) + i
            s = jnp.where(blk * c + scol < WINDOW, NEG_INF, s)
        scores.append(s)
    maxes = [jnp.maximum(jnp.max(s, axis=-1, keepdims=True), sks[kvh]) for s, (i, kvh) in zip(scores, units)]
    exps = [jnp.exp(s - m).astype(BF16) for s, m in zip(scores, maxes)]
    outs = [lax.dot_general(e, vd[kvh, i * c:i * c + span, :], NN, preferred_element_type=F32)
            for e, (i, kvh) in zip(exps, units)]
    for oe, m, (i, kvh) in zip(outs, maxes, units):
        o = oe[:, :LANES] / (oe[:, LANES:] + jnp.exp(sks[kvh] - m))
        for t in range(ATT_GROUP // 2):
            h = kvh * ATT_GROUP + 2 * t
            pair_out = jnp.where(low, o[2 * t * c:(2 * t + 1) * c], o[(2 * t + 1) * c:(2 * t + 2) * c])
            o_ref[i * c:(i + 1) * c, (h // 2) * LANES:(h // 2 + 1) * LANES] = pair_out
    kd[:, 0:WINDOW, :] = kd[:, tq:tq + WINDOW, :]
    vd[:, 0:WINDOW, :] = vd[:, tq:tq + WINDOW, :]

    @pl.when(j == pl.num_programs(1) - 1)
    def _():
        knew_ref[...] = ktail[...]
        vnew_ref[...] = vtail[...]


def _attention(qkv, cos_f, sin_a, sin_b, past_k, past_v, sink, tq, has_past):
    bsz, seq, _ = qkv.shape
    nq = Q_W // KV_W
    tab = pl.BlockSpec((tq, KV_W), lambda b, j: (j, 0))
    past = pl.BlockSpec((None, WINDOW, KV_W), lambda b, j: (b, 0, 0))
    return pl.pallas_call(
        functools.partial(_attn_kernel, tq=tq, has_past=has_past),
        grid=(bsz, seq // tq),
        in_specs=[
            pl.BlockSpec(memory_space=pltpu.SMEM),
            pl.BlockSpec((None, tq, Q_W), lambda b, j: (b, j, 0)),
            pl.BlockSpec((None, tq, KV_W), lambda b, j: (b, j, nq)),
            pl.BlockSpec((None, tq, KV_W), lambda b, j: (b, j, nq + 1)),
            tab, tab, tab, past, past,
        ],
        out_specs=[pl.BlockSpec((None, tq, Q_W), lambda b, j: (b, j, 0)), past, past],
        out_shape=[
            jax.ShapeDtypeStruct((bsz, seq, Q_W), F32),
            jax.ShapeDtypeStruct((bsz, WINDOW, KV_W), F32),
            jax.ShapeDtypeStruct((bsz, WINDOW, KV_W), F32),
        ],
        scratch_shapes=[
            pltpu.VMEM((ATT_KV_HEADS, WINDOW + tq, LANES), BF16),
            pltpu.VMEM((ATT_KV_HEADS, WINDOW + tq, 2 * LANES), BF16),
            pltpu.VMEM((WINDOW, KV_W), F32),
            pltpu.VMEM((WINDOW, KV_W), F32),
        ],
        compiler_params=_params(("parallel", "arbitrary")),
        name="swa_attention",
    )(sink, qkv, qkv, qkv, cos_f, sin_a, sin_b, past_k, past_v)


def _rope_tables(pos0, seq, rot_dim, theta):
    half = rot_dim // 2
    inv = np.exp(np.arange(half, dtype=np.float64) * (-2.0 * math.log(theta) / rot_dim))
    ang = np.arange(pos0, pos0 + seq).astype(np.float64)[:, None] * inv[None, :]
    return jnp.asarray(np.cos(ang), F32), jnp.asarray(np.sin(ang), F32)


def _largest_tile(n, cap):
    t = cap
    while n % t:
        t //= 2
    return t


def _prepare(prm):
    w = RWKV_WIDTH
    qkvg_w = 2 * RET_HEADS * RET_DK + 2 * RET_WIDTH
    cu_w = 2 * CONV_CH
    pp = {}
    pp["w_in_ab_a"] = prm["w_in_ab"][0][:, :qkvg_w].astype(BF16)
    pp["w_in_ab_b"] = prm["w_in_ab"][0][:, qkvg_w:].astype(BF16)
    pp["w_out_ab_a"] = prm["w_out_ab"][0][:RET_WIDTH].astype(BF16)
    pp["w_out_ab_b"] = prm["w_out_ab"][0][RET_WIDTH:].astype(BF16)
    pp["w_in_cd_a"] = prm["w_in_cd"][0][:, :cu_w].astype(BF16)
    pp["w_in_cd_b"] = prm["w_in_cd"][0][:, cu_w:].astype(BF16)
    pp["w_out_cd_a"] = prm["w_out_cd"][0][:CONV_CH].astype(BF16)
    pp["w_out_cd_b"] = prm["w_out_cd"][0][CONV_CH:].astype(BF16)
    for name in ("ffn_gate", "ffn_up", "ffn_down"):
        pp[name] = prm[name].astype(BF16)
    zeros = jnp.zeros((RWKV_W_LORA, w), F32)
    pp["rwkv_wwa"] = jnp.concatenate(
        [jnp.concatenate([prm["rwkv_w2"][0], zeros], axis=1), jnp.concatenate([zeros, prm["rwkv_a2"][0]], axis=1)],
        axis=0).astype(BF16)
    pp["rwkv_g2"] = prm["rwkv_g2"][0].astype(BF16)
    pp["rwkv_vecs"] = jnp.stack(
        [prm["rwkv_w0"][0], prm["rwkv_a0"][0], prm["rwkv_kk"][0], prm["rwkv_ka"][0], prm["rwkv_rk"][0].reshape(w),
         prm["rwkv_ln_g"][0], prm["rwkv_ln_b"][0], jnp.zeros((w,), F32)], axis=0)
    pp["rwkv_mu"] = prm["rwkv_mu"][0].reshape(1, RWKV_PROJ)
    head = np.arange(GROUP_LANES) // RWKV_HD
    pp["bdm"] = jnp.asarray(head[:, None] == head[None, :], BF16)
    pp["conv_w"] = jnp.concatenate([prm["conv_w"][0], jnp.zeros((CONV_PAD - CONV_WIDTH, CONV_CH), F32)], axis=0)
    pp["conv_vecs"] = jnp.concatenate(
        [jnp.stack([prm["conv_b"][0], prm["conv_ln_g"][0], prm["conv_ln_b"][0]], axis=0),
         jnp.zeros((5, CONV_CH), F32)], axis=0)
    return pp


def _rwkv_tiling(bsz, seq):
    tcb = _largest_tile(seq, RWKV_CHUNKS_PER_STEP * CHUNK)
    bb = _largest_tile(bsz, max(1, (RWKV_CHUNKS_PER_STEP * CHUNK) // tcb))
    return bb, tcb


def _trunk(x, pos0, has_past, st_ret, st_wkv, st_shift, st_conv, st_k, st_v, prm, pp):
    bsz, seq, _ = x.shape
    tokens = bsz * seq
    tm = _largest_tile(tokens, 512)
    row = lambda v: v.reshape(1, -1)

    x2 = x.reshape(tokens, D_MODEL)
    qkvg, p = _in_proj(x2, row(prm["norm_mix"][0]), pp["w_in_ab_a"], pp["w_in_ab_b"], tm)
    qkvg = qkvg.reshape(bsz, seq, -1)
    p = p.reshape(bsz, seq, RWKV_PROJ)
    cos, sin = _rope_tables(pos0, seq, RET_DK, RET_THETA)
    cos_t = jnp.concatenate([cos, cos], axis=1)
    sin_t = jnp.concatenate([-sin, sin], axis=1)
    o_ret, ret_new = _retention(qkvg, cos_t, sin_t, row(prm["ret_gn_g"][0]), st_ret, _largest_tile(seq, 256))
    h0 = st_wkv.transpose(0, 3, 1, 2).reshape(bsz, RWKV_HD, RWKV_WIDTH)
    bb, tcb = _rwkv_tiling(bsz, seq)
    o_rwkv, h_new = _rwkv(p, st_shift.reshape(bsz, 1, RWKV_PROJ), h0, pp["rwkv_mu"], pp["rwkv_vecs"],
                          pp["rwkv_wwa"], pp["rwkv_g2"], pp["bdm"], bb, tcb)
    wkv_new = h_new.reshape(bsz, RWKV_HD, RWKV_HEADS, RWKV_HD).transpose(0, 2, 3, 1)
    shift_new = p[:, -1]
    x2 = _post(x2, o_ret.reshape(tokens, -1), o_rwkv.reshape(tokens, -1), pp["w_out_ab_a"], pp["w_out_ab_b"],
               row(prm["norm_ffn"][0]), pp["ffn_gate"][0], pp["ffn_up"][0], pp["ffn_down"][0],
               row(prm["norm_final"]), tm, final=False)

    cu, qkv = _in_proj(x2, row(prm["norm_mix"][1]), pp["w_in_cd_a"], pp["w_in_cd_b"], tm)
    cu = cu.reshape(bsz, seq, -1)
    qkv = qkv.reshape(bsz, seq, -1)
    buf_pad = jnp.concatenate([jnp.zeros((bsz, CONV_PAD - (CONV_WIDTH - 1), CONV_CH), F32), st_conv], axis=1)
    o_conv, conv_new = _conv(cu, buf_pad, pp["conv_w"], pp["conv_vecs"], _largest_tile(seq, 256))
    cos, sin = _rope_tables(pos0, seq, ROT_DIM, ROPE_THETA)
    half = ROT_DIM // 2
    pad = jnp.zeros((seq, ATT_HD - ROT_DIM), F32)
    z8 = jnp.zeros((seq, half), F32)
    cos_f = jnp.concatenate([cos, cos, pad + 1.0], axis=1)
    sin_a = jnp.concatenate([-sin, z8, pad], axis=1)
    sin_b = jnp.concatenate([z8, sin, pad], axis=1)
    tile2 = lambda t: jnp.concatenate([t] * ATT_KV_HEADS, axis=1)
    o_att, k_new, v_new = _attention(qkv, tile2(cos_f), tile2(sin_a), tile2(sin_b),
                                     st_k.reshape(bsz, WINDOW, KV_W), st_v.reshape(bsz, WINDOW, KV_W),
                                     prm["attn_sink"][0], _largest_tile(seq, 256), has_past)
    y2 = _post(x2, o_conv.reshape(tokens, -1), o_att.reshape(tokens, -1), pp["w_out_cd_a"], pp["w_out_cd_b"],
               row(prm["norm_ffn"][1]), pp["ffn_gate"][1], pp["ffn_up"][1], pp["ffn_down"][1],
               row(prm["norm_final"]), tm, final=True)
    kv_shape = (bsz, WINDOW, ATT_KV_HEADS, ATT_HD)
    return (y2.reshape(bsz, seq, D_MODEL), ret_new[None], wkv_new[None], shift_new[None], conv_new[None],
            k_new.reshape(kv_shape)[None], v_new.reshape(kv_shape)[None])


def kernel(x_prompt, x_sample, state_ret, state_wkv, state_shift, state_conv, cache_k, cache_v, norm_mix, norm_ffn, norm_final, w_in_ab, w_out_ab, ret_gn_g, rwkv_mu, rwkv_w0, rwkv_w2, rwkv_a0, rwkv_a2, rwkv_g2, rwkv_kk, rwkv_ka, rwkv_rk, rwkv_ln_g, rwkv_ln_b, w_in_cd, w_out_cd, conv_w, conv_b, conv_ln_g, conv_ln_b, attn_sink, ffn_gate, ffn_up, ffn_down):
    prm = {
        'norm_mix': norm_mix, 'norm_ffn': norm_ffn, 'norm_final': norm_final,
        'w_in_ab': w_in_ab, 'w_out_ab': w_out_ab, 'ret_gn_g': ret_gn_g,
        'rwkv_mu': rwkv_mu, 'rwkv_w0': rwkv_w0, 'rwkv_w2': rwkv_w2, 'rwkv_a0': rwkv_a0, 'rwkv_a2': rwkv_a2,
        'rwkv_g2': rwkv_g2, 'rwkv_kk': rwkv_kk, 'rwkv_ka': rwkv_ka, 'rwkv_rk': rwkv_rk,
        'rwkv_ln_g': rwkv_ln_g, 'rwkv_ln_b': rwkv_ln_b,
        'w_in_cd': w_in_cd, 'w_out_cd': w_out_cd, 'conv_w': conv_w, 'conv_b': conv_b,
        'conv_ln_g': conv_ln_g, 'conv_ln_b': conv_ln_b, 'attn_sink': attn_sink,
        'ffn_gate': ffn_gate, 'ffn_up': ffn_up, 'ffn_down': ffn_down,
    }
    pp = _prepare(prm)
    bsz = x_prompt.shape[0]
    dt = x_prompt.dtype
    z_ret = jnp.zeros((bsz, RET_HEADS, RET_DK, RET_DV), dt)
    z_wkv = jnp.zeros((bsz, RWKV_HEADS, RWKV_HD, RWKV_HD), dt)
    z_shift = jnp.zeros((bsz, RWKV_PROJ), dt)
    z_conv = jnp.zeros((bsz, CONV_WIDTH - 1, CONV_CH), dt)
    z_kv = jnp.zeros((bsz, WINDOW, ATT_KV_HEADS, ATT_HD), dt)
    out_p = _trunk(x_prompt, 0, False, z_ret, z_wkv, z_shift, z_conv, z_kv, z_kv, prm, pp)
    out_s = _trunk(x_sample, PAST_LEN, True, state_ret[0], state_wkv[0], state_shift[0], state_conv[0],
                   cache_k[0], cache_v[0], prm, pp)
    return (out_p[0], out_s[0]) + tuple(out_p[1:]) + tuple(out_s[1:])
```
